```python
import jax
import jax.numpy as jnp
from jax import lax
import numpy as np


D_MODEL = 4096
BATCH = 1
SEQ = 16384
DEPTH = 4

GRID_W = 64
CTX_LEN = 256
HEAD_DIM = 128
N_HEADS_A = 16
N_KV_A = 4
N_HEADS_B = 8
WIN_H_MAX = 8
WIN_W = 16
N_GROUPS_C = 8
CHUNK = 128
Q_BLOCK = 128
D_FF = 6144
R_MOD = 256
N_MOD = 9
ROPE_THETA = 10000.0
EPS = 1e-6
W_A_Q = N_HEADS_A * HEAD_DIM
W_A_KV = N_KV_A * HEAD_DIM
W_B = N_HEADS_B * HEAD_DIM
W_C = N_GROUPS_C * HEAD_DIM
IN_COLS = W_A_Q + 2 * W_A_KV + 3 * W_B + 2 * W_C
MIX_W = W_A_Q + W_B + W_C

kernel_name = 'hybrid_dit_parallel_head_groups'


def rms_norm(x, g):
    xf = x.astype(jnp.float32)
    y = xf * lax.rsqrt(jnp.mean(xf * xf, axis=-1, keepdims=True) + EPS)
    return (y * g.astype(jnp.float32)).astype(x.dtype)


def modulated_norm(x, g, shift, scale):
    return rms_norm(x, g) * (1.0 + scale) + shift


def adaln(cond, w_down, w_up, b_up):
    m = (jax.nn.silu(cond) @ w_down) @ w_up + b_up
    return [t[:, None, :] for t in jnp.split(m, N_MOD, axis=-1)]


def ffn_half_step(x, g, shift, scale, gate, w_in, w_out):
    a, b = jnp.split(modulated_norm(x, g, shift, scale) @ w_in, 2, axis=-1)
    return x + 0.5 * gate * ((jax.nn.silu(a) * b) @ w_out)


def axial_rope_tables(n_tokens):
    t = jnp.arange(n_tokens, dtype=jnp.int32)
    row = (t // GRID_W).astype(jnp.float32)
    col = (t % GRID_W).astype(jnp.float32)
    half = HEAD_DIM // 2
    inv = ROPE_THETA ** (-jnp.arange(0, half, 2, dtype=jnp.float32) / half)
    ang_r = row[:, None] * inv[None, :]
    ang_c = col[:, None] * inv[None, :]
    ang = jnp.concatenate([ang_r, ang_r, ang_c, ang_c], axis=-1)
    return jnp.cos(ang), jnp.sin(ang)


def apply_axial_rope(x, cos, sin):
    xr = x.reshape(x.shape[:-1] + (2, 2, HEAD_DIM // 4))
    rot = jnp.stack([-xr[..., 1, :], xr[..., 0, :]], axis=-2).reshape(x.shape)
    return (x * cos[None, :, None, :] + rot * sin[None, :, None, :]).astype(x.dtype)


def neighbourhood_tables(n_tokens):
    rows = n_tokens // GRID_W
    kh = min(WIN_H_MAX, rows)
    t = jnp.arange(n_tokens, dtype=jnp.int32)
    r = t // GRID_W
    col = t % GRID_W
    r0 = jnp.clip(r - kh // 2, 0, rows - kh)
    c0 = jnp.clip(col - WIN_W // 2, 0, GRID_W - WIN_W)
    kr = r0[:, None, None] + jnp.arange(kh, dtype=jnp.int32)[None, :, None]
    kc = c0[:, None, None] + jnp.arange(WIN_W, dtype=jnp.int32)[None, None, :]
    shape = (n_tokens, kh, WIN_W)
    key_idx = jnp.broadcast_to(kr * GRID_W + kc, shape).reshape(n_tokens, kh * WIN_W)
    off_r = jnp.broadcast_to(kr - r[:, None, None] + (WIN_H_MAX - 1), shape).reshape(n_tokens, kh * WIN_W)
    off_c = jnp.broadcast_to(kc - col[:, None, None] + (WIN_W - 1), shape).reshape(n_tokens, kh * WIN_W)
    return key_idx, off_r, off_c


def heads(t, n):
    return t.reshape(t.shape[:-1] + (n, HEAD_DIM))


def split_projection(proj):
    sizes = (W_A_Q, W_A_KV, W_A_KV, W_B, W_B, W_B, W_C, W_C)
    idx = [sum(sizes[:j]) for j in range(1, len(sizes))]
    return jnp.split(proj, idx, axis=-1)


def gqa_softmax_attn(q, k, v):
    b, t, h, hd = q.shape
    g = k.shape[2]
    qg = q.reshape(b, t, g, h // g, hd)
    s = jnp.einsum('bqgrd,bkgd->bgrqk', qg, k, preferred_element_type=jnp.float32) * hd ** -0.5
    p = jax.nn.softmax(s, axis=-1).astype(v.dtype)
    o = jnp.einsum('bgrqk,bkgd->bqgrd', p, v)
    return o.reshape(b, t, h * hd)


def global_attn_blocked(q, k_all, v_all):
    b, s, h, hd = q.shape
    nblk = s // Q_BLOCK
    qb = q.reshape(b, nblk, Q_BLOCK, h, hd).swapaxes(0, 1)
    o = lax.map(lambda qi: gqa_softmax_attn(qi, k_all, v_all), qb)
    return o.swapaxes(0, 1).reshape(b, s, h * hd)


def neighbourhood_attn(q, k, v, k_ctx, v_ctx, rpb, key_idx, off_r, off_c):
    b, s, h, hd = q.shape
    nblk = s // Q_BLOCK
    n_win = key_idx.shape[-1]
    qb = q.reshape(b, nblk, Q_BLOCK, h, hd).swapaxes(0, 1)

    def blocks(a):
        return a.reshape(nblk, Q_BLOCK, n_win)

    def one_block(args):
        qi, idx, o_r, o_c = args
        kg = jnp.take(k, idx, axis=1)
        vg = jnp.take(v, idx, axis=1)
        s_win = (jnp.einsum('bqhd,bqkhd->bhqk', qi, kg, preferred_element_type=jnp.float32) * hd ** -0.5
                 + rpb[:, o_r, o_c].astype(jnp.float32))
        s_ctx = jnp.einsum('bqhd,bkhd->bhqk', qi, k_ctx, preferred_element_type=jnp.float32) * hd ** -0.5
        p = jax.nn.softmax(jnp.concatenate([s_win, s_ctx], axis=-1), axis=-1).astype(v.dtype)
        return (jnp.einsum('bhqk,bqkhd->bqhd', p[..., :n_win], vg)
                + jnp.einsum('bhqk,bkhd->bqhd', p[..., n_win:], v_ctx))

    o = lax.map(one_block, (qb, blocks(key_idx), blocks(off_r), blocks(off_c)))
    return o.swapaxes(0, 1).reshape(b, s, h * hd)


def spatial_gating(u, v, g_v, w_s, b_s):
    b, t, _ = u.shape
    n = t // CHUNK
    vn = rms_norm(heads(v, N_GROUPS_C), g_v.reshape(N_GROUPS_C, HEAD_DIM))
    vc = vn.reshape(b, n, CHUNK, N_GROUPS_C, HEAD_DIM)
    sv = jnp.einsum('gpq,bnqgc->bnpgc', w_s, vc) + b_s.T[:, :, None]
    return (u.reshape(b, n, CHUNK, N_GROUPS_C, HEAD_DIM) * sv).reshape(b, t, W_C)


def hybrid_mixer(h, hc, w_in, w_out, g_q, g_k, rpb, g_v, w_s, b_s,
                 rope_cos, rope_sin, key_idx, off_r, off_c, need_ctx_out):
    qa, ka, va, qb, kb, vb, u, v = split_projection(h @ w_in)
    qa_c, ka_c, va_c, qb_c, kb_c, vb_c, u_c, v_c = split_projection(hc @ w_in)
    qa = apply_axial_rope(rms_norm(heads(qa, N_HEADS_A), g_q), rope_cos, rope_sin)
    ka = apply_axial_rope(rms_norm(heads(ka, N_KV_A), g_k), rope_cos, rope_sin)
    ka_c = rms_norm(heads(ka_c, N_KV_A), g_k)
    va_c = heads(va_c, N_KV_A)
    k_all = jnp.concatenate([ka, ka_c], axis=1)
    v_all = jnp.concatenate([heads(va, N_KV_A), va_c], axis=1)
    o_a = global_attn_blocked(qa, k_all, v_all)
    kb_c = heads(kb_c, N_HEADS_B)
    vb_c = heads(vb_c, N_HEADS_B)
    o_b = neighbourhood_attn(heads(qb, N_HEADS_B), heads(kb, N_HEADS_B), heads(vb, N_HEADS_B),
                             kb_c, vb_c, rpb, key_idx, off_r, off_c)
    o_c = spatial_gating(jax.nn.gelu(u), jax.nn.gelu(v), g_v, w_s, b_s)
    y = jnp.concatenate([o_a, o_b, o_c], axis=-1) @ w_out
    if not need_ctx_out:
        return y, None
    o_a_c = gqa_softmax_attn(rms_norm(heads(qa_c, N_HEADS_A), g_q), ka_c, va_c)
    o_b_c = gqa_softmax_attn(heads(qb_c, N_HEADS_B), kb_c, vb_c)
    o_c_c = spatial_gating(jax.nn.gelu(u_c), jax.nn.gelu(v_c), g_v, w_s, b_s)
    yc = jnp.concatenate([o_a_c, o_b_c, o_c_c], axis=-1) @ w_out
    return y, yc


def setup_inputs(seed: int = 0) -> dict:
    key = jax.random.key(seed)
    ks = jax.random.split(key, 18)

    def nrm(k, shape, s):
        return jax.random.normal(k, shape, jnp.float32) * s

    return {
        'x': nrm(ks[0], (BATCH, SEQ, D_MODEL), 1.0),
        'c': nrm(ks[1], (BATCH, D_MODEL), 1.0),
        'ctx': nrm(ks[2], (BATCH, CTX_LEN, D_MODEL), 1.0),
        'c_ctx': nrm(ks[3], (D_MODEL,), 1.0),
        'norm_g': 1.0 + nrm(ks[4], (DEPTH, 3, D_MODEL), 0.02),
        'mod_w_down': nrm(ks[5], (DEPTH, D_MODEL, R_MOD), D_MODEL ** -0.5),
        'mod_w_up': nrm(ks[6], (DEPTH, R_MOD, N_MOD * D_MODEL), 0.5 * R_MOD ** -0.5),
        'mod_b': nrm(ks[7], (DEPTH, N_MOD * D_MODEL), 0.02),
        'ffn_w_in': nrm(ks[8], (DEPTH, 2, D_MODEL, 2 * D_FF), D_MODEL ** -0.5),
        'ffn_w_out': nrm(ks[9], (DEPTH, 2, D_FF, D_MODEL), D_FF ** -0.5),
        'mix_w_in': nrm(ks[10], (DEPTH, D_MODEL, IN_COLS), D_MODEL ** -0.5),
        'mix_w_out': nrm(ks[11], (DEPTH, MIX_W, D_MODEL), MIX_W ** -0.5),
        'qk_norm_g': 1.0 + nrm(ks[12], (DEPTH, 2, HEAD_DIM), 0.02),
        'rpb': nrm(ks[13], (DEPTH, N_HEADS_B, 2 * WIN_H_MAX - 1, 2 * WIN_W - 1), 0.1),
        'sgu_norm_g': 1.0 + nrm(ks[14], (DEPTH, W_C), 0.02),
        'sgu_w': nrm(ks[15], (DEPTH, N_GROUPS_C, CHUNK, CHUNK), CHUNK ** -0.5),
        'sgu_b': 1.0 + nrm(ks[16], (DEPTH, N_GROUPS_C, CHUNK), 0.02),
        'final_norm_g': 1.0 + nrm(ks[17], (D_MODEL,), 0.02),
    }


def reference(x, c, ctx, c_ctx, norm_g, mod_w_down, mod_w_up, mod_b, ffn_w_in, ffn_w_out,
              mix_w_in, mix_w_out, qk_norm_g, rpb, sgu_norm_g, sgu_w, sgu_b, final_norm_g):
    n_tok = x.shape[1]
    rope_cos, rope_sin = axial_rope_tables(n_tok)
    key_idx, off_r, off_c = neighbourhood_tables(n_tok)
    xc = ctx
    for i in range(DEPTH):
        last = i == DEPTH - 1
        m = adaln(c, mod_w_down[i], mod_w_up[i], mod_b[i])
        mc = adaln(c_ctx[None, :], mod_w_down[i], mod_w_up[i], mod_b[i])
        x = ffn_half_step(x, norm_g[i, 0], m[0], m[1], m[2], ffn_w_in[i, 0], ffn_w_out[i, 0])
        xc = ffn_half_step(xc, norm_g[i, 0], mc[0], mc[1], mc[2], ffn_w_in[i, 0], ffn_w_out[i, 0])
        h = modulated_norm(x, norm_g[i, 1], m[3], m[4])
        hc = modulated_norm(xc, norm_g[i, 1], mc[3], mc[4])
        y, yc = hybrid_mixer(h, hc, mix_w_in[i], mix_w_out[i], qk_norm_g[i, 0], qk_norm_g[i, 1],
                             rpb[i], sgu_norm_g[i], sgu_w[i], sgu_b[i], rope_cos, rope_sin,
                             key_idx, off_r, off_c, not last)
        x = x + m[5] * y
        x = ffn_half_step(x, norm_g[i, 2], m[6], m[7], m[8], ffn_w_in[i, 1], ffn_w_out[i, 1])
        if not last:
            xc = xc + mc[5] * yc
            xc = ffn_half_step(xc, norm_g[i, 2], mc[6], mc[7], mc[8], ffn_w_in[i, 1], ffn_w_out[i, 1])
    return rms_norm(x, final_norm_g)
```

```python
import functools

import jax
import jax.numpy as jnp
from jax import lax
from jax.experimental import pallas as pl
from jax.experimental.pallas import tpu as pltpu

HEAD_DIM = 128
GRID_W = 64
N_HEADS_A = 16
N_KV_A = 4
N_HEADS_B = 8
N_GROUPS_C = 8
WIN_H = 8
WIN_W = 16
CHUNK = 128
N_MOD = 9
ROPE_THETA = 10000.0
EPS = 1e-6
W_A_Q = N_HEADS_A * HEAD_DIM
W_A_KV = N_KV_A * HEAD_DIM
W_B = N_HEADS_B * HEAD_DIM
W_C = N_GROUPS_C * HEAD_DIM
GQA_REP = N_HEADS_A // N_KV_A

NBR_Q = 2 * GRID_W
NBR_KV_BLOCKS = 5
NBR_KEYS = NBR_KV_BLOCKS * NBR_Q
NBR_PATTERNS = 6
MASK_VALUE = -1e30

V7X_VMEM_BYTES = 64 * 1024 * 1024
VMEM_LIMIT_BYTES = V7X_VMEM_BYTES - 8 * 1024 * 1024
MOD_ROWS = 8

F32 = jnp.float32
BF16 = jnp.bfloat16


def _tile(n, target, align):
    best = None
    for t in range(align, min(n, target) + 1, align):
        if n % t == 0:
            best = t
    assert best is not None, (n, target, align)
    return best


def _params(*sem):
    return pltpu.CompilerParams(dimension_semantics=sem, vmem_limit_bytes=VMEM_LIMIT_BYTES)


def _row_select(mod_ref, tm, n_lat):
    row = pl.program_id(0) * tm + lax.broadcasted_iota(jnp.int32, (tm, 1), 0)
    return jnp.where(row >= n_lat, mod_ref[1:2, :], mod_ref[0:1, :])


def _dot(a, b):
    return jnp.dot(a, b, preferred_element_type=F32)


def _adaln_kernel(cond_ref, wd_ref, wu_ref, b_ref, o_ref):
    s = cond_ref[...]
    s = s * (1.0 / (1.0 + jnp.exp(-s)))
    mid = jnp.dot(s, wd_ref[...], preferred_element_type=F32, precision=lax.Precision.HIGHEST)
    o_ref[...] = jnp.dot(mid, wu_ref[...], preferred_element_type=F32,
                         precision=lax.Precision.HIGHEST) + b_ref[...]


def _adaln(cond, w_down, w_up, b_up):
    depth, d, r = w_down.shape
    n = w_up.shape[-1]
    tn = _tile(n, 4096, 128)
    return pl.pallas_call(
        _adaln_kernel,
        grid=(depth, n // tn),
        in_specs=[
            pl.BlockSpec((MOD_ROWS, d), lambda l, j: (0, 0)),
            pl.BlockSpec((None, d, r), lambda l, j: (l, 0, 0)),
            pl.BlockSpec((None, r, tn), lambda l, j: (l, 0, j)),
            pl.BlockSpec((None, 1, tn), lambda l, j: (l, 0, j)),
        ],
        out_specs=pl.BlockSpec((None, MOD_ROWS, tn), lambda l, j: (l, 0, j)),
        out_shape=jax.ShapeDtypeStruct((depth, MOD_ROWS, n), F32),
        compiler_params=_params("parallel", "parallel"),
        name="adaln",
    )(cond, w_down, w_up, b_up.reshape(depth, 1, n))


def _norm_mod_kernel(x_ref, g_ref, shift_ref, scale_ref, o_ref, *, tr, n_lat):
    x = x_ref[...]
    y = x * lax.rsqrt(jnp.mean(x * x, axis=-1, keepdims=True) + EPS) * g_ref[...]
    scale = _row_select(scale_ref, tr, n_lat)
    shift = _row_select(shift_ref, tr, n_lat)
    o_ref[...] = (y * (1.0 + scale) + shift).astype(o_ref.dtype)


def _norm_mod(x, norm_g, mods, layer, sub, n_lat):
    n, d = x.shape
    tr = _tile(n, 256, 8)
    kern = functools.partial(_norm_mod_kernel, tr=tr, n_lat=n_lat)
    return pl.pallas_call(
        kern,
        grid=(n // tr,),
        in_specs=[
            pl.BlockSpec((tr, d), lambda i: (i, 0)),
            pl.BlockSpec((None, None, 1, d), lambda i: (layer, sub, 0, 0)),
            pl.BlockSpec((None, None, MOD_ROWS, d), lambda i: (layer, 3 * sub, 0, 0)),
            pl.BlockSpec((None, None, MOD_ROWS, d), lambda i: (layer, 3 * sub + 1, 0, 0)),
        ],
        out_specs=pl.BlockSpec((tr, d), lambda i: (i, 0)),
        out_shape=jax.ShapeDtypeStruct((n, d), BF16),
        compiler_params=_params("parallel"),
        name="norm_mod",
    )(x, norm_g, mods, mods)


def _final_norm_kernel(x_ref, g_ref, o_ref):
    x = x_ref[...]
    o_ref[...] = x * lax.rsqrt(jnp.mean(x * x, axis=-1, keepdims=True) + EPS) * g_ref[...]


def _final_norm(x, g, n_lat):
    d = x.shape[1]
    tr = _tile(n_lat, 256, 8)
    return pl.pallas_call(
        _final_norm_kernel,
        grid=(n_lat // tr,),
        in_specs=[pl.BlockSpec((tr, d), lambda i: (i, 0)),
                  pl.BlockSpec((1, d), lambda i: (0, 0))],
        out_specs=pl.BlockSpec((tr, d), lambda i: (i, 0)),
        out_shape=jax.ShapeDtypeStruct((n_lat, d), F32),
        compiler_params=_params("parallel"),
        name="final_norm",
    )(x, g.reshape(1, d))


def _ffn_up_kernel(h_ref, wa_ref, wb_ref, o_ref):
    h = h_ref[...]
    a = _dot(h, wa_ref[...])
    b = _dot(h, wb_ref[...])
    o_ref[...] = (a * (1.0 / (1.0 + jnp.exp(-a))) * b).astype(o_ref.dtype)


def _ffn_up(h, w_in, layer, half):
    n, d = h.shape
    f = w_in.shape[-1] // 2
    tm = _tile(n, 640, 128)
    tn = _tile(f, 512, 128)
    nb = f // tn
    return pl.pallas_call(
        _ffn_up_kernel,
        grid=(n // tm, nb),
        in_specs=[
            pl.BlockSpec((tm, d), lambda i, j: (i, 0)),
            pl.BlockSpec((None, None, d, tn), lambda i, j: (layer, half, 0, j)),
            pl.BlockSpec((None, None, d, tn), lambda i, j: (layer, half, 0, j + nb)),
        ],
        out_specs=pl.BlockSpec((tm, tn), lambda i, j: (i, j)),
        out_shape=jax.ShapeDtypeStruct((n, f), BF16),
        compiler_params=_params("parallel", "parallel"),
        name="ffn_up",
    )(h, w_in, w_in)


def _ffn_down_kernel(g_ref, w_ref, x_ref, gate_ref, o_ref, *, tm, n_lat):
    y = _dot(g_ref[...], w_ref[...])
    o_ref[...] = x_ref[...] + 0.5 * _row_select(gate_ref, tm, n_lat) * y


def _ffn_down(g, w_out, x, mods, layer, half, gate_idx, n_lat):
    n, f = g.shape
    d = x.shape[1]
    tm = _tile(n, 640, 128)
    tn = _tile(d, 512, 128)
    kern = functools.partial(_ffn_down_kernel, tm=tm, n_lat=n_lat)
    return pl.pallas_call(
        kern,
        grid=(n // tm, d // tn),
        in_specs=[
            pl.BlockSpec((tm, f), lambda i, j: (i, 0)),
            pl.BlockSpec((None, None, f, tn), lambda i, j: (layer, half, 0, j)),
            pl.BlockSpec((tm, tn), lambda i, j: (i, j)),
            pl.BlockSpec((None, None, MOD_ROWS, tn), lambda i, j: (layer, gate_idx, 0, j)),
        ],
        out_specs=pl.BlockSpec((tm, tn), lambda i, j: (i, j)),
        out_shape=jax.ShapeDtypeStruct((n, d), F32),
        compiler_params=_params("parallel", "parallel"),
        name="ffn_down",
    )(g, w_out, x, mods)


def _head_rms(x, g):
    return x * lax.rsqrt(jnp.mean(x * x, axis=-1, keepdims=True) + EPS) * g


def _rope(x, cos, sin_signed):
    lane = lax.broadcasted_iota(jnp.int32, x.shape, 1)
    low = jnp.bitwise_and(lane, HEAD_DIM // 2 - 1) < (HEAD_DIM // 4)
    rot = jnp.where(low, pltpu.roll(x, HEAD_DIM - HEAD_DIM // 4, 1), pltpu.roll(x, HEAD_DIM // 4, 1))
    return x * cos + rot * sin_signed


def _epi_qk_rope(acc, g_ref, cos_ref, sin_ref, *, out_scale):
    g = g_ref[...]
    cos = cos_ref[...]
    sin = sin_ref[...]
    heads = []
    for h in range(acc.shape[1] // HEAD_DIM):
        x = _head_rms(acc[:, h * HEAD_DIM:(h + 1) * HEAD_DIM], g)
        heads.append(_rope(x, cos, sin) * out_scale)
    return jnp.concatenate(heads, axis=1)


def _epi_plain(acc):
    return acc


def _gelu_tanh(x):
    return 0.5 * x * (1.0 + jnp.tanh(0.7978845608028654 * (x + 0.044715 * (x * x * x))))


def _epi_gelu(acc):
    return _gelu_tanh(acc)


def _epi_gelu_rms(acc, g_ref):
    y = _gelu_tanh(acc)
    g = g_ref[...]
    groups = []
    for c in range(acc.shape[1] // HEAD_DIM):
        sl = slice(c * HEAD_DIM, (c + 1) * HEAD_DIM)
        groups.append(_head_rms(y[:, sl], g[:, sl]))
    return jnp.concatenate(groups, axis=1)


def _proj_kernel(h_ref, w_ref, *rest, epilogue):
    extras, o_ref = rest[:-1], rest[-1]
    o_ref[...] = epilogue(_dot(h_ref[...], w_ref[...]), *extras).astype(o_ref.dtype)


def _proj(h, w_in, layer, col_off, n_cols, epilogue, extras, extra_specs, out_dtype, name):
    n, d = h.shape
    tm = _tile(n, 640, 128)
    tn = _tile(n_cols, 512, 128)
    assert col_off % tn == 0
    off = col_off // tn
    return pl.pallas_call(
        functools.partial(_proj_kernel, epilogue=epilogue),
        grid=(n // tm, n_cols // tn),
        in_specs=[
            pl.BlockSpec((tm, d), lambda i, j: (i, 0)),
            pl.BlockSpec((None, d, tn), lambda i, j: (layer, 0, j + off)),
        ] + [spec(tm, tn) for spec in extra_specs],
        out_specs=pl.BlockSpec((tm, tn), lambda i, j: (i, j)),
        out_shape=jax.ShapeDtypeStruct((n, n_cols), out_dtype),
        compiler_params=_params("parallel", "parallel"),
        name=name,
    )(h, w_in, *extras)


def _gattn_kernel(q_ref, k_ref, v_ref, o_ref, m_ref, l_ref, acc_ref, *, tq, tk, n_lat, n_ctx):
    q = q_ref[...]
    qs = jnp.concatenate([q[:, h * HEAD_DIM:(h + 1) * HEAD_DIM] for h in range(GQA_REP)], axis=0)
    m_ref[...] = jnp.full(m_ref.shape, MASK_VALUE, F32)
    l_ref[...] = jnp.zeros(l_ref.shape, F32)
    acc_ref[...] = jnp.zeros(acc_ref.shape, F32)

    def step(k, v):
        s = lax.dot_general(qs, k, (((1,), (1,)), ((), ())), preferred_element_type=F32)
        m_old = m_ref[...]
        m_new = jnp.maximum(m_old, jnp.max(s, axis=1, keepdims=True))
        alpha = jnp.exp(m_old - m_new)
        p = jnp.exp(s - m_new)
        l_ref[...] = alpha * l_ref[...] + jnp.sum(p, axis=1, keepdims=True)
        acc_ref[...] = alpha * acc_ref[...] + _dot(p.astype(BF16), v)
        m_ref[...] = m_new

    def body(c, carry):
        off = pl.multiple_of(c * tk, tk)
        step(k_ref[pl.ds(off, tk), :], v_ref[pl.ds(off, tk), :])
        return carry

    is_ctx = pl.program_id(1) * tq >= n_lat
    lax.fori_loop(0, jnp.where(is_ctx, 0, n_lat // tk), body, 0)
    step(k_ref[n_lat:n_lat + n_ctx, :], v_ref[n_lat:n_lat + n_ctx, :])

    o = acc_ref[...] / l_ref[...]
    for h in range(GQA_REP):
        o_ref[:, h * HEAD_DIM:(h + 1) * HEAD_DIM] = o[h * tq:(h + 1) * tq].astype(o_ref.dtype)


def _global_attn(qa, ka, va, n_lat, n_ctx):
    n = qa.shape[0]
    tq = _tile(n_ctx, 256, 8)
    assert n_lat % tq == 0 and n_ctx % tq == 0
    tk = _tile(n_lat, 512, 128)
    gw = GQA_REP * HEAD_DIM
    kern = functools.partial(_gattn_kernel, tq=tq, tk=tk, n_lat=n_lat, n_ctx=n_ctx)
    return pl.pallas_call(
        kern,
        grid=(N_KV_A, n // tq),
        in_specs=[
            pl.BlockSpec((tq, gw), lambda g, i: (i, g)),
            pl.BlockSpec((n, HEAD_DIM), lambda g, i: (0, g)),
            pl.BlockSpec((n, HEAD_DIM), lambda g, i: (0, g)),
        ],
        out_specs=pl.BlockSpec((tq, gw), lambda g, i: (i, g)),
        out_shape=jax.ShapeDtypeStruct((n, W_A_Q), BF16),
        scratch_shapes=[
            pltpu.VMEM((GQA_REP * tq, 1), F32),
            pltpu.VMEM((GQA_REP * tq, 1), F32),
            pltpu.VMEM((GQA_REP * tq, HEAD_DIM), F32),
        ],
        compiler_params=_params("parallel", "parallel"),
        name="global_attn",
    )(qa, ka, va)


def _nbr_kernel(q_ref, *rest):
    k_refs = rest[:NBR_KV_BLOCKS]
    v_refs = rest[NBR_KV_BLOCKS:2 * NBR_KV_BLOCKS]
    kc_ref, vc_ref, bias_ref, o_ref = rest[2 * NBR_KV_BLOCKS:]
    scale = HEAD_DIM ** -0.5
    nt = (((1,), (1,)), ((), ()))
    for h in range(N_HEADS_B):
        sl = slice(h * HEAD_DIM, (h + 1) * HEAD_DIM)
        q = q_ref[:, sl]
        k = jnp.concatenate([r[:, sl] for r in k_refs], axis=0)
        v = jnp.concatenate([r[:, sl] for r in v_refs], axis=0)
        s_win = lax.dot_general(q, k, nt, preferred_element_type=F32) * scale + bias_ref[h]
        s_ctx = lax.dot_general(q, kc_ref[:, sl], nt, preferred_element_type=F32) * scale
        m = jnp.maximum(jnp.max(s_win, axis=1, keepdims=True), jnp.max(s_ctx, axis=1, keepdims=True))
        p_win = jnp.exp(s_win - m)
        p_ctx = jnp.exp(s_ctx - m)
        l = jnp.sum(p_win, axis=1, keepdims=True) + jnp.sum(p_ctx, axis=1, keepdims=True)
        o = _dot(p_win.astype(BF16), v) + _dot(p_ctx.astype(BF16), vc_ref[:, sl])
        o_ref[:, sl] = (o / l).astype(o_ref.dtype)


def _nbr_bias_table(rpb, n_lat):
    rows = n_lat // GRID_W
    nblk = n_lat // NBR_Q
    kh = min(WIN_H, rows)
    t = jnp.arange(NBR_Q, dtype=jnp.int32)
    kk = jnp.arange(NBR_KEYS, dtype=jnp.int32)
    q_r, q_c = t // GRID_W, t % GRID_W
    k_r, k_c = kk // GRID_W, kk % GRID_W
    tables = []
    for blk in (0, 1, 2, nblk - 2, nblk - 1):
        blk0 = min(max(blk - 2, 0), nblk - NBR_KV_BLOCKS)
        rq = (2 * blk + q_r)[:, None]
        kr = (2 * blk0 + k_r)[None, :]
        r0 = jnp.clip(rq - kh // 2, 0, rows - kh)
        c0 = jnp.clip(q_c - WIN_W // 2, 0, GRID_W - WIN_W)[:, None]
        kc = k_c[None, :]
        inside = (kr >= r0) & (kr < r0 + kh) & (kc >= c0) & (kc < c0 + WIN_W)
        off_r = jnp.clip(kr - rq + (WIN_H - 1), 0, 2 * WIN_H - 2)
        off_c = jnp.clip(kc - q_c[:, None] + (WIN_W - 1), 0, 2 * WIN_W - 2)
        tables.append(jnp.where(inside[None], rpb[:, off_r, off_c].astype(F32), MASK_VALUE))
    tables.append(jnp.full_like(tables[0], MASK_VALUE))
    return jnp.stack(tables)


def _nbr_attn(qkv_b, bias, n_lat, n_ctx):
    n = qkv_b.shape[0]
    nblk = n_lat // NBR_Q
    assert nblk >= NBR_KV_BLOCKS + 1 and n_ctx % NBR_Q == 0
    ctx_blk = n_lat // n_ctx
    assert n_lat % n_ctx == 0

    def first_kv(i):
        return jnp.clip(i - 2, 0, nblk - NBR_KV_BLOCKS)

    def pattern(i):
        return jnp.where(i >= nblk, NBR_PATTERNS - 1, i - first_kv(i))

    def kv_spec(col, b):
        return pl.BlockSpec((NBR_Q, W_B), lambda i: (first_kv(i) + b, col))

    in_specs = ([pl.BlockSpec((NBR_Q, W_B), lambda i: (i, 0))]
                + [kv_spec(1, b) for b in range(NBR_KV_BLOCKS)]
                + [kv_spec(2, b) for b in range(NBR_KV_BLOCKS)]
                + [pl.BlockSpec((n_ctx, W_B), lambda i: (ctx_blk, 1)),
                   pl.BlockSpec((n_ctx, W_B), lambda i: (ctx_blk, 2)),
                   pl.BlockSpec((None, N_HEADS_B, NBR_Q, NBR_KEYS), lambda i: (pattern(i), 0, 0, 0))])
    return pl.pallas_call(
        _nbr_kernel,
        grid=(n // NBR_Q,),
        in_specs=in_specs,
        out_specs=pl.BlockSpec((NBR_Q, W_B), lambda i: (i, 0)),
        out_shape=jax.ShapeDtypeStruct((n, W_B), BF16),
        compiler_params=_params("parallel"),
        name="nbr_attn",
    )(*([qkv_b] * (3 + 2 * NBR_KV_BLOCKS)), bias)


def _sgu_kernel(u_ref, vn_ref, w_ref, bt_ref, o_ref, *, n_chunks):
    for g in range(N_GROUPS_C):
        sl = slice(g * HEAD_DIM, (g + 1) * HEAD_DIM)
        w = w_ref[g]
        b = bt_ref[:, g:g + 1]
        for c in range(n_chunks):
            rs = slice(c * CHUNK, (c + 1) * CHUNK)
            sv = _dot(w, vn_ref[rs, sl]) + b
            o_ref[rs, sl] = (u_ref[rs, sl] * sv).astype(o_ref.dtype)


def _sgu(u, vn, w_s, b_t, layer):
    n = u.shape[0]
    tr = _tile(n, 1280, CHUNK)
    kern = functools.partial(_sgu_kernel, n_chunks=tr // CHUNK)
    return pl.pallas_call(
        kern,
        grid=(n // tr,),
        in_specs=[
            pl.BlockSpec((tr, W_C), lambda i: (i, 0)),
            pl.BlockSpec((tr, W_C), lambda i: (i, 0)),
            pl.BlockSpec((None, N_GROUPS_C, CHUNK, CHUNK), lambda i: (layer, 0, 0, 0)),
            pl.BlockSpec((None, CHUNK, N_GROUPS_C), lambda i: (layer, 0, 0)),
        ],
        out_specs=pl.BlockSpec((tr, W_C), lambda i: (i, 0)),
        out_shape=jax.ShapeDtypeStruct((n, W_C), BF16),
        compiler_params=_params("parallel"),
        name="sgu",
    )(u, vn, w_s, b_t)


def _out_proj_kernel(oa_ref, ob_ref, oc_ref, wa_ref, wb_ref, wc_ref, x_ref, gate_ref, o_ref, *, tm, n_lat):
    y = _dot(oa_ref[...], wa_ref[...]) + _dot(ob_ref[...], wb_ref[...]) + _dot(oc_ref[...], wc_ref[...])
    o_ref[...] = x_ref[...] + _row_select(gate_ref, tm, n_lat) * y


def _out_proj(oa, ob, oc, w_out, x, mods, layer, n_lat):
    n, d = x.shape
    tm = _tile(n, 640, 128)
    tn = _tile(d, 512, 128)
    assert W_A_Q % W_B == 0 and W_B == W_C
    kern = functools.partial(_out_proj_kernel, tm=tm, n_lat=n_lat)
    return pl.pallas_call(
        kern,
        grid=(n // tm, d // tn),
        in_specs=[
            pl.BlockSpec((tm, W_A_Q), lambda i, j: (i, 0)),
            pl.BlockSpec((tm, W_B), lambda i, j: (i, 0)),
            pl.BlockSpec((tm, W_C), lambda i, j: (i, 0)),
            pl.BlockSpec((None, W_A_Q, tn), lambda i, j: (layer, 0, j)),
            pl.BlockSpec((None, W_B, tn), lambda i, j: (layer, W_A_Q // W_B, j)),
            pl.BlockSpec((None, W_C, tn), lambda i, j: (layer, W_A_Q // W_B + 1, j)),
            pl.BlockSpec((tm, tn), lambda i, j: (i, j)),
            pl.BlockSpec((None, None, MOD_ROWS, tn), lambda i, j: (layer, 5, 0, j)),
        ],
        out_specs=pl.BlockSpec((tm, tn), lambda i, j: (i, j)),
        out_shape=jax.ShapeDtypeStruct((n, d), F32),
        compiler_params=_params("parallel", "parallel"),
        name="out_proj",
    )(oa, ob, oc, w_out, w_out, w_out, x, mods)


def _rope_tables(n_lat, n_ctx):
    t = jnp.arange(n_lat, dtype=jnp.int32)
    row = (t // GRID_W).astype(F32)
    col = (t % GRID_W).astype(F32)
    half = HEAD_DIM // 2
    inv = ROPE_THETA ** (-jnp.arange(0, half, 2, dtype=F32) / half)
    ang_r = row[:, None] * inv[None, :]
    ang_c = col[:, None] * inv[None, :]
    ang = jnp.concatenate([ang_r, ang_r, ang_c, ang_c], axis=-1)
    lane = jnp.arange(HEAD_DIM)
    sign = jnp.where((lane % half) < HEAD_DIM // 4, -1.0, 1.0).astype(F32)
    cos = jnp.concatenate([jnp.cos(ang), jnp.ones((n_ctx, HEAD_DIM), F32)], axis=0)
    sin = jnp.concatenate([jnp.sin(ang) * sign, jnp.zeros((n_ctx, HEAD_DIM), F32)], axis=0)
    return cos, sin


def kernel(x, c, ctx, c_ctx, norm_g, mod_w_down, mod_w_up, mod_b, ffn_w_in, ffn_w_out, mix_w_in, mix_w_out,
           qk_norm_g, rpb, sgu_norm_g, sgu_w, sgu_b, final_norm_g):
    batch, n_lat, d = x.shape
    n_ctx = ctx.shape[1]
    depth = norm_g.shape[0]
    assert batch == 1 and c.shape[0] == 1

    ffn_w_in_b = ffn_w_in.astype(BF16)
    ffn_w_out_b = ffn_w_out.astype(BF16)
    mix_w_in_b = mix_w_in.astype(BF16)
    mix_w_out_b = mix_w_out.astype(BF16)
    sgu_w_b = sgu_w.astype(BF16)
    sgu_b_t = jnp.swapaxes(sgu_b, 1, 2)
    norm_g4 = norm_g.reshape(depth, 3, 1, d)

    cond = jnp.zeros((MOD_ROWS, d), F32).at[0].set(c[0]).at[1].set(c_ctx)
    mods = _adaln(cond, mod_w_down, mod_w_up, mod_b)
    mods = mods.reshape(depth, MOD_ROWS, N_MOD, d).swapaxes(1, 2)

    cos, sin = _rope_tables(n_lat, n_ctx)
    rope_specs = [lambda tm, tn: pl.BlockSpec((tm, HEAD_DIM), lambda i, j: (i, 0))] * 2

    def g_spec(layer, which):
        return lambda tm, tn: pl.BlockSpec((None, None, 1, HEAD_DIM), lambda i, j: (layer, which, 0, 0))

    qk_g = qk_norm_g.reshape(depth, 2, 1, HEAD_DIM)
    sgu_g = sgu_norm_g.reshape(depth, 1, W_C)

    xs = jnp.concatenate([x[0], ctx[0]], axis=0)
    col_ka = W_A_Q
    col_va = col_ka + W_A_KV
    col_b = col_va + W_A_KV
    col_u = col_b + 3 * W_B
    col_v = col_u + W_C

    for l in range(depth):
        h = _norm_mod(xs, norm_g4, mods, l, 0, n_lat)
        xs = _ffn_down(_ffn_up(h, ffn_w_in_b, l, 0), ffn_w_out_b, xs, mods, l, 0, 2, n_lat)

        h = _norm_mod(xs, norm_g4, mods, l, 1, n_lat)
        qa = _proj(h, mix_w_in_b, l, 0, W_A_Q,
                   functools.partial(_epi_qk_rope, out_scale=HEAD_DIM ** -0.5),
                   [qk_g, cos, sin], [g_spec(l, 0)] + rope_specs, BF16, "proj_qa")
        ka = _proj(h, mix_w_in_b, l, col_ka, W_A_KV,
                   functools.partial(_epi_qk_rope, out_scale=1.0),
                   [qk_g, cos, sin], [g_spec(l, 1)] + rope_specs, BF16, "proj_ka")
        va = _proj(h, mix_w_in_b, l, col_va, W_A_KV, _epi_plain, [], [], BF16, "proj_va")
        qkv_b = _proj(h, mix_w_in_b, l, col_b, 3 * W_B, _epi_plain, [], [], BF16, "proj_b")
        u = _proj(h, mix_w_in_b, l, col_u, W_C, _epi_gelu, [], [], F32, "proj_u")
        vn = _proj(h, mix_w_in_b, l, col_v, W_C, _epi_gelu_rms, [sgu_g],
                   [lambda tm, tn, l=l: pl.BlockSpec((None, 1, tn), lambda i, j: (l, 0, j))], BF16, "proj_v")

        oa = _global_attn(qa, ka, va, n_lat, n_ctx)
        ob = _nbr_attn(qkv_b, _nbr_bias_table(rpb[l], n_lat), n_lat, n_ctx)
        oc = _sgu(u, vn, sgu_w_b, sgu_b_t, l)
        xs = _out_proj(oa, ob, oc, mix_w_out_b, xs, mods, l, n_lat)

        h = _norm_mod(xs, norm_g4, mods, l, 2, n_lat)
        xs = _ffn_down(_ffn_up(h, ffn_w_in_b, l, 1), ffn_w_out_b, xs, mods, l, 1, 8, n_lat)

    return _final_norm(xs, final_norm_g, n_lat).reshape(1, n_lat, d)
```

```python
import functools

import jax
import jax.numpy as jnp
import numpy as np
from jax import lax
from jax.experimental import pallas as pl
from jax.experimental.pallas import tpu as pltpu

HEAD_DIM = 128
GRID_W = 64
N_HEADS_A = 16
N_KV_A = 4
N_HEADS_B = 8
N_GROUPS_C = 8
WIN_H = 8
WIN_W = 16
CHUNK = 128
N_MOD = 9
ROPE_THETA = 10000.0
EPS = 1e-6
W_A_Q = N_HEADS_A * HEAD_DIM
W_A_KV = N_KV_A * HEAD_DIM
W_B = N_HEADS_B * HEAD_DIM
W_C = N_GROUPS_C * HEAD_DIM
GQA_REP = N_HEADS_A // N_KV_A

NBR_Q = 2 * GRID_W
NBR_KV_BLOCKS = 5
NBR_KEYS = NBR_KV_BLOCKS * NBR_Q
NBR_PATTERNS = 6
MASK_VALUE = -1e30
LOG2_E = 1.4426950408889634

V7X_VMEM_BYTES = 64 * 1024 * 1024
VMEM_LIMIT_BYTES = V7X_VMEM_BYTES - 8 * 1024 * 1024
MOD_ROWS = 8

F32 = jnp.float32
BF16 = jnp.bfloat16


def _tile(n, target, align):
    best = None
    for t in range(align, min(n, target) + 1, align):
        if n % t == 0:
            best = t
    assert best is not None, (n, target, align)
    return best


def _params(*sem):
    return pltpu.CompilerParams(dimension_semantics=sem, vmem_limit_bytes=VMEM_LIMIT_BYTES)


def _row_select(mod_ref, tm, n_lat):
    row = pl.program_id(0) * tm + lax.broadcasted_iota(jnp.int32, (tm, 1), 0)
    return jnp.where(row >= n_lat, mod_ref[1:2, :], mod_ref[0:1, :])


def _dot(a, b):
    return jnp.dot(a, b, preferred_element_type=F32)


def _adaln_kernel(cond_ref, wd_ref, wu_ref, b_ref, o_ref):
    s = cond_ref[...]
    s = s * (1.0 / (1.0 + jnp.exp(-s)))
    mid = jnp.dot(s, wd_ref[...], preferred_element_type=F32, precision=lax.Precision.HIGHEST)
    o_ref[...] = jnp.dot(mid, wu_ref[...], preferred_element_type=F32,
                         precision=lax.Precision.HIGHEST) + b_ref[...]


def _adaln(cond, w_down, w_up, b_up):
    depth, d, r = w_down.shape
    n = w_up.shape[-1]
    tn = _tile(n, 4096, 128)
    return pl.pallas_call(
        _adaln_kernel,
        grid=(depth, n // tn),
        in_specs=[
            pl.BlockSpec((MOD_ROWS, d), lambda l, j: (0, 0)),
            pl.BlockSpec((None, d, r), lambda l, j: (l, 0, 0)),
            pl.BlockSpec((None, r, tn), lambda l, j: (l, 0, j)),
            pl.BlockSpec((None, 1, tn), lambda l, j: (l, 0, j)),
        ],
        out_specs=pl.BlockSpec((None, MOD_ROWS, tn), lambda l, j: (l, 0, j)),
        out_shape=jax.ShapeDtypeStruct((depth, MOD_ROWS, n), F32),
        compiler_params=_params("parallel", "parallel"),
        name="adaln",
    )(cond, w_down, w_up, b_up.reshape(depth, 1, n))


def _norm_mod_kernel(x_ref, g_ref, shift_ref, scale_ref, o_ref, *, tr, n_lat):
    x = x_ref[...]
    y = x * lax.rsqrt(jnp.mean(x * x, axis=-1, keepdims=True) + EPS) * g_ref[...]
    scale = _row_select(scale_ref, tr, n_lat)
    shift = _row_select(shift_ref, tr, n_lat)
    o_ref[...] = (y * (1.0 + scale) + shift).astype(o_ref.dtype)


def _norm_mod(x, norm_g, mods, layer, sub, n_lat):
    n, d = x.shape
    tr = _tile(n, 256, 8)
    kern = functools.partial(_norm_mod_kernel, tr=tr, n_lat=n_lat)
    return pl.pallas_call(
        kern,
        grid=(n // tr,),
        in_specs=[
            pl.BlockSpec((tr, d), lambda i: (i, 0)),
            pl.BlockSpec((None, None, 1, d), lambda i: (layer, sub, 0, 0)),
            pl.BlockSpec((None, None, MOD_ROWS, d), lambda i: (layer, 3 * sub, 0, 0)),
            pl.BlockSpec((None, None, MOD_ROWS, d), lambda i: (layer, 3 * sub + 1, 0, 0)),
        ],
        out_specs=pl.BlockSpec((tr, d), lambda i: (i, 0)),
        out_shape=jax.ShapeDtypeStruct((n, d), BF16),
        compiler_params=_params("parallel"),
        name="norm_mod",
    )(x, norm_g, mods, mods)


def _final_norm_kernel(x_ref, g_ref, o_ref):
    x = x_ref[...]
    o_ref[...] = x * lax.rsqrt(jnp.mean(x * x, axis=-1, keepdims=True) + EPS) * g_ref[...]


def _final_norm(x, g, n_lat):
    d = x.shape[1]
    tr = _tile(n_lat, 256, 8)
    return pl.pallas_call(
        _final_norm_kernel,
        grid=(n_lat // tr,),
        in_specs=[pl.BlockSpec((tr, d), lambda i: (i, 0)),
                  pl.BlockSpec((1, d), lambda i: (0, 0))],
        out_specs=pl.BlockSpec((tr, d), lambda i: (i, 0)),
        out_shape=jax.ShapeDtypeStruct((n_lat, d), F32),
        compiler_params=_params("parallel"),
        name="final_norm",
    )(x, g.reshape(1, d))


def _ffn_up_kernel(h_ref, wa_ref, wb_ref, o_ref):
    h = h_ref[...]
    a = _dot(h, wa_ref[...])
    b = _dot(h, wb_ref[...])
    o_ref[...] = (a * (1.0 / (1.0 + jnp.exp(-a))) * b).astype(o_ref.dtype)


def _ffn_up(h, w_in, layer, half):
    n, d = h.shape
    f = w_in.shape[-1] // 2
    tm = _tile(n, 640, 128)
    tn = _tile(f, 512, 128)
    nb = f // tn
    return pl.pallas_call(
        _ffn_up_kernel,
        grid=(n // tm, nb),
        in_specs=[
            pl.BlockSpec((tm, d), lambda i, j: (i, 0)),
            pl.BlockSpec((None, None, d, tn), lambda i, j: (layer, half, 0, j)),
            pl.BlockSpec((None, None, d, tn), lambda i, j: (layer, half, 0, j + nb)),
        ],
        out_specs=pl.BlockSpec((tm, tn), lambda i, j: (i, j)),
        out_shape=jax.ShapeDtypeStruct((n, f), BF16),
        compiler_params=_params("parallel", "parallel"),
        name="ffn_up",
    )(h, w_in, w_in)


def _ffn_down_kernel(g_ref, w_ref, x_ref, gate_ref, o_ref, *, tm, n_lat):
    y = _dot(g_ref[...], w_ref[...])
    o_ref[...] = x_ref[...] + 0.5 * _row_select(gate_ref, tm, n_lat) * y


def _ffn_down(g, w_out, x, mods, layer, half, gate_idx, n_lat):
    n, f = g.shape
    d = x.shape[1]
    tm = _tile(n, 640, 128)
    tn = _tile(d, 512, 128)
    kern = functools.partial(_ffn_down_kernel, tm=tm, n_lat=n_lat)
    return pl.pallas_call(
        kern,
        grid=(n // tm, d // tn),
        in_specs=[
            pl.BlockSpec((tm, f), lambda i, j: (i, 0)),
            pl.BlockSpec((None, None, f, tn), lambda i, j: (layer, half, 0, j)),
            pl.BlockSpec((tm, tn), lambda i, j: (i, j)),
            pl.BlockSpec((None, None, MOD_ROWS, tn), lambda i, j: (layer, gate_idx, 0, j)),
        ],
        out_specs=pl.BlockSpec((tm, tn), lambda i, j: (i, j)),
        out_shape=jax.ShapeDtypeStruct((n, d), F32),
        compiler_params=_params("parallel", "parallel"),
        name="ffn_down",
    )(g, w_out, x, mods)


def _head_rms(x, g):
    return x * lax.rsqrt(jnp.mean(x * x, axis=-1, keepdims=True) + EPS) * g


def _rope(x, cos, sin_signed):
    lane = lax.broadcasted_iota(jnp.int32, x.shape, 1)
    low = jnp.bitwise_and(lane, HEAD_DIM // 2 - 1) < (HEAD_DIM // 4)
    rot = jnp.where(low, pltpu.roll(x, HEAD_DIM - HEAD_DIM // 4, 1), pltpu.roll(x, HEAD_DIM // 4, 1))
    return x * cos + rot * sin_signed


def _epi_qk_rope(acc, g_ref, cos_ref, sin_ref, *, out_scale):
    g = g_ref[...]
    cos = cos_ref[...]
    sin = sin_ref[...]
    heads = []
    for h in range(acc.shape[1] // HEAD_DIM):
        x = _head_rms(acc[:, h * HEAD_DIM:(h + 1) * HEAD_DIM], g)
        heads.append(_rope(x, cos, sin) * out_scale)
    return jnp.concatenate(heads, axis=1)


def _epi_plain(acc):
    return acc


def _gelu_tanh(x):
    return 0.5 * x * (1.0 + jnp.tanh(0.7978845608028654 * (x + 0.044715 * (x * x * x))))


def _epi_gelu(acc):
    return _gelu_tanh(acc)


def _epi_gelu_rms(acc, g_ref):
    y = _gelu_tanh(acc)
    g = g_ref[...]
    groups = []
    for c in range(acc.shape[1] // HEAD_DIM):
        sl = slice(c * HEAD_DIM, (c + 1) * HEAD_DIM)
        groups.append(_head_rms(y[:, sl], g[:, sl]))
    return jnp.concatenate(groups, axis=1)


def _proj_kernel(h_ref, w_ref, *rest, epilogue):
    extras, o_ref = rest[:-1], rest[-1]
    o_ref[...] = epilogue(_dot(h_ref[...], w_ref[...]), *extras).astype(o_ref.dtype)


def _proj(h, w_in, layer, col_off, n_cols, epilogue, extras, extra_specs, out_dtype, name):
    n, d = h.shape
    tm = _tile(n, 640, 128)
    tn = _tile(n_cols, 512, 128)
    assert col_off % tn == 0
    off = col_off // tn
    return pl.pallas_call(
        functools.partial(_proj_kernel, epilogue=epilogue),
        grid=(n // tm, n_cols // tn),
        in_specs=[
            pl.BlockSpec((tm, d), lambda i, j: (i, 0)),
            pl.BlockSpec((None, d, tn), lambda i, j: (layer, 0, j + off)),
        ] + [spec(tm, tn) for spec in extra_specs],
        out_specs=pl.BlockSpec((tm, tn), lambda i, j: (i, j)),
        out_shape=jax.ShapeDtypeStruct((n, n_cols), out_dtype),
        compiler_params=_params("parallel", "parallel"),
        name=name,
    )(h, w_in, *extras)


def _gattn_kernel(q_ref, kt_ref, v_ref, o_ref, qs_ref, m_ref, l_ref, acc_ref, *, tq, tk, n_lat, n_ctx):
    for h in range(GQA_REP):
        qs_ref[h * tq:(h + 1) * tq, :] = q_ref[:, h * HEAD_DIM:(h + 1) * HEAD_DIM]
    m_ref[...] = jnp.full(m_ref.shape, MASK_VALUE, F32)
    l_ref[...] = jnp.zeros(l_ref.shape, F32)
    acc_ref[...] = jnp.zeros(acc_ref.shape, F32)

    def step(kt, v):
        ncol = v.shape[0] // HEAD_DIM
        s = _dot(qs_ref[...], kt)
        m_old = m_ref[...]
        m_new = jnp.maximum(m_old, jnp.max(s, axis=1, keepdims=True))
        alpha = jnp.exp2(m_old - m_new)
        p = jnp.exp2(s - jnp.concatenate([m_new] * ncol, axis=1))
        psum = p[:, :HEAD_DIM]
        for j in range(1, ncol):
            psum = psum + p[:, j * HEAD_DIM:(j + 1) * HEAD_DIM]
        l_ref[...] = alpha * l_ref[...] + psum
        acc_ref[...] = alpha * acc_ref[...] + _dot(p.astype(BF16), v)
        m_ref[...] = m_new

    def body(c, carry):
        off = pl.multiple_of(c * tk, tk)
        step(kt_ref[:, pl.ds(off, tk)], v_ref[pl.ds(off, tk), :])
        return carry

    is_ctx = pl.program_id(1) * tq >= n_lat
    lax.fori_loop(0, jnp.where(is_ctx, 0, n_lat // tk), body, 0)
    step(kt_ref[:, n_lat:n_lat + n_ctx], v_ref[n_lat:n_lat + n_ctx, :])

    o = acc_ref[...] / jnp.sum(l_ref[...], axis=1, keepdims=True)
    for h in range(GQA_REP):
        o_ref[:, h * HEAD_DIM:(h + 1) * HEAD_DIM] = o[h * tq:(h + 1) * tq].astype(o_ref.dtype)


def _global_attn(qa, ka_t, va, n_lat, n_ctx):
    n = qa.shape[0]
    tq = _tile(n_ctx, 256, 8)
    assert n_lat % tq == 0 and n_ctx % tq == 0
    tk = _tile(n_lat, 2048, 128)
    gw = GQA_REP * HEAD_DIM
    kern = functools.partial(_gattn_kernel, tq=tq, tk=tk, n_lat=n_lat, n_ctx=n_ctx)
    return pl.pallas_call(
        kern,
        grid=(N_KV_A, n // tq),
        in_specs=[
            pl.BlockSpec((tq, gw), lambda g, i: (i, g)),
            pl.BlockSpec((HEAD_DIM, n), lambda g, i: (g, 0)),
            pl.BlockSpec((n, HEAD_DIM), lambda g, i: (0, g)),
        ],
        out_specs=pl.BlockSpec((tq, gw), lambda g, i: (i, g)),
        out_shape=jax.ShapeDtypeStruct((n, W_A_Q), BF16),
        scratch_shapes=[
            pltpu.VMEM((GQA_REP * tq, HEAD_DIM), BF16),
            pltpu.VMEM((GQA_REP * tq, HEAD_DIM), F32),
            pltpu.VMEM((GQA_REP * tq, HEAD_DIM), F32),
            pltpu.VMEM((GQA_REP * tq, HEAD_DIM), F32),
        ],
        compiler_params=_params("parallel", "parallel"),
        name="global_attn",
    )(qa, ka_t, va)


def _nbr_kernel(q_ref, *rest):
    k_refs = rest[:NBR_KV_BLOCKS]
    v_refs = rest[NBR_KV_BLOCKS:2 * NBR_KV_BLOCKS]
    kc_ref, vc_ref, bias_ref, o_ref = rest[2 * NBR_KV_BLOCKS:]
    scale = HEAD_DIM ** -0.5
    nt = (((1,), (1,)), ((), ()))
    for h in range(N_HEADS_B):
        sl = slice(h * HEAD_DIM, (h + 1) * HEAD_DIM)
        q = q_ref[:, sl]
        k = jnp.concatenate([r[:, sl] for r in k_refs], axis=0)
        v = jnp.concatenate([r[:, sl] for r in v_refs], axis=0)
        s_win = lax.dot_general(q, k, nt, preferred_element_type=F32) * scale + bias_ref[h]
        s_ctx = lax.dot_general(q, kc_ref[:, sl], nt, preferred_element_type=F32) * scale
        m = jnp.maximum(jnp.max(s_win, axis=1, keepdims=True), jnp.max(s_ctx, axis=1, keepdims=True))
        p_win = jnp.exp(s_win - m)
        p_ctx = jnp.exp(s_ctx - m)
        l = jnp.sum(p_win, axis=1, keepdims=True) + jnp.sum(p_ctx, axis=1, keepdims=True)
        o = _dot(p_win.astype(BF16), v) + _dot(p_ctx.astype(BF16), vc_ref[:, sl])
        o_ref[:, sl] = (o / l).astype(o_ref.dtype)


def _nbr_bias_tables(rpb, n_lat):
    rows = n_lat // GRID_W
    nblk = n_lat // NBR_Q
    kh = min(WIN_H, rows)
    q_rows, kv_rows = NBR_Q // GRID_W, NBR_KEYS // GRID_W
    n_off_r, n_off_c = 2 * WIN_H - 1, 2 * WIN_W - 1
    blocks = (0, 1, 2, nblk - 2, nblk - 1)
    sel_r = np.zeros((len(blocks), n_off_r, q_rows, kv_rows), np.float32)
    for p, blk in enumerate(blocks):
        blk0 = min(max(blk - 2, 0), nblk - NBR_KV_BLOCKS)
        for qr in range(q_rows):
            rq = q_rows * blk + qr
            r0 = min(max(rq - kh // 2, 0), rows - kh)
            for kr in range(kv_rows):
                key_row = q_rows * blk0 + kr
                if r0 <= key_row < r0 + kh:
                    sel_r[p, key_row - rq + WIN_H - 1, qr, kr] = 1.0
    sel_c = np.zeros((n_off_c, GRID_W, GRID_W), np.float32)
    for qc in range(GRID_W):
        c0 = min(max(qc - WIN_W // 2, 0), GRID_W - WIN_W)
        for kc in range(c0, c0 + WIN_W):
            sel_c[kc - qc + WIN_W - 1, qc, kc] = 1.0
    inside = np.einsum("paqk,bcd->pqckd", sel_r, sel_c).reshape(len(blocks), NBR_Q, NBR_KEYS) > 0.5
    bias = jnp.einsum("lhab,paqk,bcd->lphqckd", rpb.astype(F32), sel_r, sel_c,
                      precision=lax.Precision.HIGHEST)
    bias = bias.reshape(rpb.shape[0], len(blocks), N_HEADS_B, NBR_Q, NBR_KEYS)
    bias = jnp.where(inside[None, :, None], bias, MASK_VALUE)
    return jnp.concatenate([bias, jnp.full_like(bias[:, :1], MASK_VALUE)], axis=1)


def _nbr_attn(qkv_b, bias, layer, n_lat, n_ctx):
    n = qkv_b.shape[0]
    nblk = n_lat // NBR_Q
    assert nblk >= NBR_KV_BLOCKS + 1 and n_ctx % NBR_Q == 0
    ctx_blk = n_lat // n_ctx
    assert n_lat % n_ctx == 0

    def first_kv(i):
        return jnp.clip(i - 2, 0, nblk - NBR_KV_BLOCKS)

    def pattern(i):
        return jnp.where(i >= nblk, NBR_PATTERNS - 1, i - first_kv(i))

    def kv_spec(col, b):
        return pl.BlockSpec((NBR_Q, W_B), lambda i: (first_kv(i) + b, col))

    in_specs = ([pl.BlockSpec((NBR_Q, W_B), lambda i: (i, 0))]
                + [kv_spec(1, b) for b in range(NBR_KV_BLOCKS)]
                + [kv_spec(2, b) for b in range(NBR_KV_BLOCKS)]
                + [pl.BlockSpec((n_ctx, W_B), lambda i: (ctx_blk, 1)),
                   pl.BlockSpec((n_ctx, W_B), lambda i: (ctx_blk, 2)),
                   pl.BlockSpec((None, None, N_HEADS_B, NBR_Q, NBR_KEYS),
                                lambda i: (layer, pattern(i), 0, 0, 0))])
    return pl.pallas_call(
        _nbr_kernel,
        grid=(n // NBR_Q,),
        in_specs=in_specs,
        out_specs=pl.BlockSpec((NBR_Q, W_B), lambda i: (i, 0)),
        out_shape=jax.ShapeDtypeStruct((n, W_B), BF16),
        compiler_params=_params("parallel"),
        name="nbr_attn",
    )(*([qkv_b] * (3 + 2 * NBR_KV_BLOCKS)), bias)


def _sgu_kernel(u_ref, vn_ref, w_ref, bt_ref, o_ref, *, n_chunks):
    for g in range(N_GROUPS_C):
        sl = slice(g * HEAD_DIM, (g + 1) * HEAD_DIM)
        w = w_ref[g]
        b = bt_ref[:, g:g + 1]
        for c in range(n_chunks):
            rs = slice(c * CHUNK, (c + 1) * CHUNK)
            sv = _dot(w, vn_ref[rs, sl]) + b
            o_ref[rs, sl] = (u_ref[rs, sl] * sv).astype(o_ref.dtype)


def _sgu(u, vn, w_s, b_t, layer):
    n = u.shape[0]
    tr = _tile(n, 1280, CHUNK)
    kern = functools.partial(_sgu_kernel, n_chunks=tr // CHUNK)
    return pl.pallas_call(
        kern,
        grid=(n // tr,),
        in_specs=[
            pl.BlockSpec((tr, W_C), lambda i: (i, 0)),
            pl.BlockSpec((tr, W_C), lambda i: (i, 0)),
            pl.BlockSpec((None, N_GROUPS_C, CHUNK, CHUNK), lambda i: (layer, 0, 0, 0)),
            pl.BlockSpec((None, CHUNK, N_GROUPS_C), lambda i: (layer, 0, 0)),
        ],
        out_specs=pl.BlockSpec((tr, W_C), lambda i: (i, 0)),
        out_shape=jax.ShapeDtypeStruct((n, W_C), BF16),
        compiler_params=_params("parallel"),
        name="sgu",
    )(u, vn, w_s, b_t)


def _out_proj_kernel(oa_ref, ob_ref, oc_ref, wa_ref, wb_ref, wc_ref, x_ref, gate_ref, o_ref, *, tm, n_lat):
    y = _dot(oa_ref[...], wa_ref[...]) + _dot(ob_ref[...], wb_ref[...]) + _dot(oc_ref[...], wc_ref[...])
    o_ref[...] = x_ref[...] + _row_select(gate_ref, tm, n_lat) * y


def _out_proj(oa, ob, oc, w_out, x, mods, layer, n_lat):
    n, d = x.shape
    tm = _tile(n, 640, 128)
    tn = _tile(d, 512, 128)
    assert W_A_Q % W_B == 0 and W_B == W_C
    kern = functools.partial(_out_proj_kernel, tm=tm, n_lat=n_lat)
    return pl.pallas_call(
        kern,
        grid=(n // tm, d // tn),
        in_specs=[
            pl.BlockSpec((tm, W_A_Q), lambda i, j: (i, 0)),
            pl.BlockSpec((tm, W_B), lambda i, j: (i, 0)),
            pl.BlockSpec((tm, W_C), lambda i, j: (i, 0)),
            pl.BlockSpec((None, W_A_Q, tn), lambda i, j: (layer, 0, j)),
            pl.BlockSpec((None, W_B, tn), lambda i, j: (layer, W_A_Q // W_B, j)),
            pl.BlockSpec((None, W_C, tn), lambda i, j: (layer, W_A_Q // W_B + 1, j)),
            pl.BlockSpec((tm, tn), lambda i, j: (i, j)),
            pl.BlockSpec((None, None, MOD_ROWS, tn), lambda i, j: (layer, 5, 0, j)),
        ],
        out_specs=pl.BlockSpec((tm, tn), lambda i, j: (i, j)),
        out_shape=jax.ShapeDtypeStruct((n, d), F32),
        compiler_params=_params("parallel", "parallel"),
        name="out_proj",
    )(oa, ob, oc, w_out, w_out, w_out, x, mods)


def _rope_tables(n_lat, n_ctx):
    t = jnp.arange(n_lat, dtype=jnp.int32)
    row = (t // GRID_W).astype(F32)
    col = (t % GRID_W).astype(F32)
    half = HEAD_DIM // 2
    inv = ROPE_THETA ** (-jnp.arange(0, half, 2, dtype=F32) / half)
    ang_r = row[:, None] * inv[None, :]
    ang_c = col[:, None] * inv[None, :]
    ang = jnp.concatenate([ang_r, ang_r, ang_c, ang_c], axis=-1)
    lane = jnp.arange(HEAD_DIM)
    sign = jnp.where((lane % half) < HEAD_DIM // 4, -1.0, 1.0).astype(F32)
    cos = jnp.concatenate([jnp.cos(ang), jnp.ones((n_ctx, HEAD_DIM), F32)], axis=0)
    sin = jnp.concatenate([jnp.sin(ang) * sign, jnp.zeros((n_ctx, HEAD_DIM), F32)], axis=0)
    return cos, sin


def kernel(x, c, ctx, c_ctx, norm_g, mod_w_down, mod_w_up, mod_b, ffn_w_in, ffn_w_out, mix_w_in, mix_w_out,
           qk_norm_g, rpb, sgu_norm_g, sgu_w, sgu_b, final_norm_g):
    batch, n_lat, d = x.shape
    n_ctx = ctx.shape[1]
    depth = norm_g.shape[0]
    assert batch == 1 and c.shape[0] == 1

    ffn_w_in_b = ffn_w_in.astype(BF16)
    ffn_w_out_b = ffn_w_out.astype(BF16)
    mix_w_in_b = mix_w_in.astype(BF16)
    mix_w_out_b = mix_w_out.astype(BF16)
    sgu_w_b = sgu_w.astype(BF16)
    sgu_b_t = jnp.swapaxes(sgu_b, 1, 2)
    norm_g4 = norm_g.reshape(depth, 3, 1, d)

    cond = jnp.zeros((MOD_ROWS, d), F32).at[0].set(c[0]).at[1].set(c_ctx)
    mods = _adaln(cond, mod_w_down, mod_w_up, mod_b)
    mods = mods.reshape(depth, MOD_ROWS, N_MOD, d).swapaxes(1, 2)

    cos, sin = _rope_tables(n_lat, n_ctx)
    nbr_bias = _nbr_bias_tables(rpb, n_lat)
    rope_specs = [lambda tm, tn: pl.BlockSpec((tm, HEAD_DIM), lambda i, j: (i, 0))] * 2

    def g_spec(layer, which):
        return lambda tm, tn: pl.BlockSpec((None, None, 1, HEAD_DIM), lambda i, j: (layer, which, 0, 0))

    qk_g = qk_norm_g.reshape(depth, 2, 1, HEAD_DIM)
    sgu_g = sgu_norm_g.reshape(depth, 1, W_C)

    xs = jnp.concatenate([x[0], ctx[0]], axis=0)
    col_ka = W_A_Q
    col_va = col_ka + W_A_KV
    col_b = col_va + W_A_KV
    col_u = col_b + 3 * W_B
    col_v = col_u + W_C

    for l in range(depth):
        h = _norm_mod(xs, norm_g4, mods, l, 0, n_lat)
        xs = _ffn_down(_ffn_up(h, ffn_w_in_b, l, 0), ffn_w_out_b, xs, mods, l, 0, 2, n_lat)

        h = _norm_mod(xs, norm_g4, mods, l, 1, n_lat)
        qa = _proj(h, mix_w_in_b, l, 0, W_A_Q,
                   functools.partial(_epi_qk_rope, out_scale=HEAD_DIM ** -0.5 * LOG2_E),
                   [qk_g, cos, sin], [g_spec(l, 0)] + rope_specs, BF16, "proj_qa")
        ka = _proj(h, mix_w_in_b, l, col_ka, W_A_KV,
                   functools.partial(_epi_qk_rope, out_scale=1.0),
                   [qk_g, cos, sin], [g_spec(l, 1)] + rope_specs, BF16, "proj_ka")
        va = _proj(h, mix_w_in_b, l, col_va, W_A_KV, _epi_plain, [], [], BF16, "proj_va")
        qkv_b = _proj(h, mix_w_in_b, l, col_b, 3 * W_B, _epi_plain, [], [], BF16, "proj_b")
        u = _proj(h, mix_w_in_b, l, col_u, W_C, _epi_gelu, [], [], F32, "proj_u")
        vn = _proj(h, mix_w_in_b, l, col_v, W_C, _epi_gelu_rms, [sgu_g],
                   [lambda tm, tn, l=l: pl.BlockSpec((None, 1, tn), lambda i, j: (l, 0, j))], BF16, "proj_v")

        oa = _global_attn(qa, ka.T, va, n_lat, n_ctx)
        ob = _nbr_attn(qkv_b, nbr_bias, l, n_lat, n_ctx)
        oc = _sgu(u, vn, sgu_w_b, sgu_b_t, l)
        xs = _out_proj(oa, ob, oc, mix_w_out_b, xs, mods, l, n_lat)

        h = _norm_mod(xs, norm_g4, mods, l, 2, n_lat)
        xs = _ffn_down(_ffn_up(h, ffn_w_in_b, l, 1), ffn_w_out_b, xs, mods, l, 1, 8, n_lat)

    return _final_norm(xs, final_norm_g, n_lat).reshape(1, n_lat, d)
```

```python
import functools
import math

import jax
import jax.numpy as jnp
import numpy as np
from jax import lax
from jax.experimental import pallas as pl
from jax.experimental.pallas import tpu as pltpu

HEAD_DIM = 128
GRID_W = 64
N_HEADS_A = 16
N_KV_A = 4
N_HEADS_B = 8
N_GROUPS_C = 8
WIN_H = 8
WIN_W = 16
CHUNK = 128
N_MOD = 9
ROPE_THETA = 10000.0
EPS = 1e-6
W_A_Q = N_HEADS_A * HEAD_DIM
W_A_KV = N_KV_A * HEAD_DIM
W_B = N_HEADS_B * HEAD_DIM
W_C = N_GROUPS_C * HEAD_DIM
GQA_REP = N_HEADS_A // N_KV_A

NBR_Q = 2 * GRID_W
NBR_KV_BLOCKS = 5
NBR_KEYS = NBR_KV_BLOCKS * NBR_Q
NBR_PATTERNS = 6
MASK_VALUE = -1e30
LOG2_E = 1.4426950408889634

V7X_VMEM_BYTES = 64 * 1024 * 1024
VMEM_LIMIT_BYTES = V7X_VMEM_BYTES - 8 * 1024 * 1024
V7X_MXU_COLS = 256
MOD_ROWS = 8
NORM_ROWS = 16

F32 = jnp.float32
BF16 = jnp.bfloat16


def _tile(n, target, align):
    best = None
    for t in range(align, min(n, target) + 1, align):
        if n % t == 0:
            best = t
    assert best is not None, (n, target, align)
    return best


def _params(*sem):
    return pltpu.CompilerParams(dimension_semantics=sem, vmem_limit_bytes=VMEM_LIMIT_BYTES)


def _row_select(mod_ref, tm, n_lat):
    row = pl.program_id(0) * tm + lax.broadcasted_iota(jnp.int32, (tm, 1), 0)
    return jnp.where(row >= n_lat, mod_ref[1:2, :], mod_ref[0:1, :])


def _dot(a, b):
    return jnp.dot(a, b, preferred_element_type=F32)


def _adaln_kernel(cond_ref, wd_ref, wu_ref, b_ref, o_ref, mid_ref):
    @pl.when(pl.program_id(1) == 0)
    def _():
        s = cond_ref[...]
        s = s * (1.0 / (1.0 + jnp.exp(-s)))
        mid_ref[...] = jnp.dot(s, wd_ref[...], preferred_element_type=F32, precision=lax.Precision.HIGHEST)

    o_ref[...] = jnp.dot(mid_ref[...], wu_ref[...], preferred_element_type=F32,
                         precision=lax.Precision.HIGHEST) + b_ref[...]


def _adaln(cond, w_down, w_up, b_up):
    depth, d, r = w_down.shape
    n = w_up.shape[-1]
    tn = _tile(n, 4096, 128)
    return pl.pallas_call(
        _adaln_kernel,
        grid=(depth, n // tn),
        in_specs=[
            pl.BlockSpec((MOD_ROWS, d), lambda l, j: (0, 0)),
            pl.BlockSpec((None, d, r), lambda l, j: (l, 0, 0)),
            pl.BlockSpec((None, r, tn), lambda l, j: (l, 0, j)),
            pl.BlockSpec((None, 1, tn), lambda l, j: (l, 0, j)),
        ],
        out_specs=pl.BlockSpec((None, MOD_ROWS, tn), lambda l, j: (l, 0, j)),
        out_shape=jax.ShapeDtypeStruct((depth, MOD_ROWS, n), F32),
        scratch_shapes=[pltpu.VMEM((MOD_ROWS, r), F32)],
        compiler_params=_params("parallel", "arbitrary"),
        name="adaln",
    )(cond, w_down, w_up, b_up.reshape(depth, 1, n))


def _norm_mod_kernel(x_ref, g_ref, shift_ref, scale_ref, o_ref, *, tr, n_lat):
    which = (pl.program_id(0) * tr >= n_lat).astype(jnp.int32)
    gain = g_ref[...] * (1.0 + scale_ref[pl.ds(which, 1), :])
    shift = shift_ref[pl.ds(which, 1), :]

    def body(r, carry):
        rs = pl.ds(pl.multiple_of(r * NORM_ROWS, NORM_ROWS), NORM_ROWS)
        x = x_ref[rs, :]
        inv = lax.rsqrt(jnp.mean(x * x, axis=-1, keepdims=True) + EPS)
        o_ref[rs, :] = (x_ref[rs, :] * inv * gain + shift).astype(o_ref.dtype)
        return carry

    lax.fori_loop(0, tr // NORM_ROWS, body, 0, unroll=2)


def _norm_mod(x, norm_g, mods, layer, sub, n_lat):
    n, d = x.shape
    tr = _tile(math.gcd(n_lat, n - n_lat), 256, NORM_ROWS)
    kern = functools.partial(_norm_mod_kernel, tr=tr, n_lat=n_lat)
    return pl.pallas_call(
        kern,
        grid=(n // tr,),
        in_specs=[
            pl.BlockSpec((tr, d), lambda i: (i, 0)),
            pl.BlockSpec((None, None, 1, d), lambda i: (layer, sub, 0, 0)),
            pl.BlockSpec((None, None, MOD_ROWS, d), lambda i: (layer, 3 * sub, 0, 0)),
            pl.BlockSpec((None, None, MOD_ROWS, d), lambda i: (layer, 3 * sub + 1, 0, 0)),
        ],
        out_specs=pl.BlockSpec((tr, d), lambda i: (i, 0)),
        out_shape=jax.ShapeDtypeStruct((n, d), BF16),
        compiler_params=_params("parallel"),
        name="norm_mod",
    )(x, norm_g, mods, mods)


def _final_norm_kernel(x_ref, g_ref, o_ref):
    g = g_ref[...]

    def body(r, carry):
        rs = pl.ds(pl.multiple_of(r * NORM_ROWS, NORM_ROWS), NORM_ROWS)
        x = x_ref[rs, :]
        inv = lax.rsqrt(jnp.mean(x * x, axis=-1, keepdims=True) + EPS)
        o_ref[rs, :] = x_ref[rs, :] * inv * g
        return carry

    lax.fori_loop(0, x_ref.shape[0] // NORM_ROWS, body, 0, unroll=2)


def _final_norm(x, g, n_lat):
    d = x.shape[1]
    tr = _tile(n_lat, 256, NORM_ROWS)
    return pl.pallas_call(
        _final_norm_kernel,
        grid=(n_lat // tr,),
        in_specs=[pl.BlockSpec((tr, d), lambda i: (i, 0)),
                  pl.BlockSpec((1, d), lambda i: (0, 0))],
        out_specs=pl.BlockSpec((tr, d), lambda i: (i, 0)),
        out_shape=jax.ShapeDtypeStruct((n_lat, d), F32),
        compiler_params=_params("parallel"),
        name="final_norm",
    )(x, g.reshape(1, d))


def _ffn_up_kernel(h_ref, wa_ref, wb_ref, o_ref):
    h = h_ref[...]
    a = _dot(h, wa_ref[...])
    b = _dot(h, wb_ref[...])
    o_ref[...] = (a * (1.0 / (1.0 + jnp.exp(-a))) * b).astype(o_ref.dtype)


def _ffn_up(h, w_in, layer, half):
    n, d = h.shape
    f = w_in.shape[-1] // 2
    tm = _tile(n, 1280, 128)
    tn = _tile(f, 512, 128)
    nb = f // tn
    return pl.pallas_call(
        _ffn_up_kernel,
        grid=(n // tm, nb),
        in_specs=[
            pl.BlockSpec((tm, d), lambda i, j: (i, 0)),
            pl.BlockSpec((None, None, d, tn), lambda i, j: (layer, half, 0, j)),
            pl.BlockSpec((None, None, d, tn), lambda i, j: (layer, half, 0, j + nb)),
        ],
        out_specs=pl.BlockSpec((tm, tn), lambda i, j: (i, j)),
        out_shape=jax.ShapeDtypeStruct((n, f), BF16),
        compiler_params=_params("parallel", "parallel"),
        name="ffn_up",
    )(h, w_in, w_in)


def _ffn_down_kernel(g_ref, w_ref, x_ref, gate_ref, o_ref, *, tm, n_lat):
    y = _dot(g_ref[...], w_ref[...])
    o_ref[...] = x_ref[...] + 0.5 * _row_select(gate_ref, tm, n_lat) * y


def _ffn_down(g, w_out, x, mods, layer, half, gate_idx, n_lat):
    n, f = g.shape
    d = x.shape[1]
    tm = _tile(n, 1280, 128)
    tn = _tile(d, 256, 128)
    kern = functools.partial(_ffn_down_kernel, tm=tm, n_lat=n_lat)
    return pl.pallas_call(
        kern,
        grid=(n // tm, d // tn),
        in_specs=[
            pl.BlockSpec((tm, f), lambda i, j: (i, 0)),
            pl.BlockSpec((None, None, f, tn), lambda i, j: (layer, half, 0, j)),
            pl.BlockSpec((tm, tn), lambda i, j: (i, j)),
            pl.BlockSpec((None, None, MOD_ROWS, tn), lambda i, j: (layer, gate_idx, 0, j)),
        ],
        out_specs=pl.BlockSpec((tm, tn), lambda i, j: (i, j)),
        out_shape=jax.ShapeDtypeStruct((n, d), F32),
        compiler_params=_params("parallel", "parallel"),
        name="ffn_down",
    )(g, w_out, x, mods)


def _head_rms(x, g):
    return x * lax.rsqrt(jnp.mean(x * x, axis=-1, keepdims=True) + EPS) * g


def _rope(x, cos, sin_signed):
    lane = lax.broadcasted_iota(jnp.int32, x.shape, 1)
    low = jnp.bitwise_and(lane, HEAD_DIM // 2 - 1) < (HEAD_DIM // 4)
    rot = jnp.where(low, pltpu.roll(x, HEAD_DIM - HEAD_DIM // 4, 1), pltpu.roll(x, HEAD_DIM // 4, 1))
    return x * cos + rot * sin_signed


def _epi_qk_rope(acc, cols, g_ref, cos_ref, sin_ref, *, out_scale):
    g = g_ref[...]
    cos = cos_ref[...]
    sin = sin_ref[...]
    heads = []
    for h in range(acc.shape[1] // HEAD_DIM):
        x = _head_rms(acc[:, h * HEAD_DIM:(h + 1) * HEAD_DIM], g)
        heads.append(_rope(x, cos, sin) * out_scale)
    return jnp.concatenate(heads, axis=1)


def _epi_plain(acc, cols):
    return acc


def _gelu_tanh(x):
    return 0.5 * x * (1.0 + jnp.tanh(0.7978845608028654 * (x + 0.044715 * (x * x * x))))


def _epi_gelu(acc, cols):
    return _gelu_tanh(acc)


def _epi_gelu_rms(acc, cols, g_ref):
    y = _gelu_tanh(acc)
    g = g_ref[:, cols]
    groups = []
    for c in range(acc.shape[1] // HEAD_DIM):
        sl = slice(c * HEAD_DIM, (c + 1) * HEAD_DIM)
        groups.append(_head_rms(y[:, sl], g[:, sl]))
    return jnp.concatenate(groups, axis=1)


def _proj_kernel(h_ref, w_ref, *rest, epilogue, chunk):
    extras, o_ref = rest[:-1], rest[-1]
    h = h_ref[...]
    for c in range(o_ref.shape[1] // chunk):
        cols = slice(c * chunk, (c + 1) * chunk)
        o_ref[:, cols] = epilogue(_dot(h, w_ref[:, cols]), cols, *extras).astype(o_ref.dtype)


def _proj(h, w_in, layer, col_off, n_cols, epilogue, extras, extra_specs, out_dtype, name, chunk=None):
    n, d = h.shape
    tm = _tile(n, 640, 128)
    tn = _tile(n_cols, 1024, 128)
    chunk = tn if chunk is None else chunk
    assert col_off % tn == 0 and tn % chunk == 0
    off = col_off // tn
    return pl.pallas_call(
        functools.partial(_proj_kernel, epilogue=epilogue, chunk=chunk),
        grid=(n // tm, n_cols // tn),
        in_specs=[
            pl.BlockSpec((tm, d), lambda i, j: (i, 0)),
            pl.BlockSpec((None, d, tn), lambda i, j: (layer, 0, j + off)),
        ] + [spec(tm, tn) for spec in extra_specs],
        out_specs=pl.BlockSpec((tm, tn), lambda i, j: (i, j)),
        out_shape=jax.ShapeDtypeStruct((n, n_cols), out_dtype),
        compiler_params=_params("parallel", "parallel"),
        name=name,
    )(h, w_in, *extras)


def _gattn_kernel(q_ref, kt_ref, v_ref, o_ref, qs_ref, m_ref, l_ref, acc_ref, *, tq, tk, n_lat, n_ctx):
    for h in range(GQA_REP):
        qs_ref[h * tq:(h + 1) * tq, :] = q_ref[:, h * HEAD_DIM:(h + 1) * HEAD_DIM]
    m_ref[...] = jnp.full(m_ref.shape, MASK_VALUE, F32)
    l_ref[...] = jnp.zeros(l_ref.shape, F32)
    acc_ref[...] = jnp.zeros(acc_ref.shape, F32)

    def step(kt, v):
        ncol = v.shape[0] // HEAD_DIM
        s = _dot(qs_ref[...], kt)
        m_old = m_ref[...]
        m_new = jnp.maximum(m_old, jnp.max(s, axis=1, keepdims=True))
        alpha = jnp.exp2(m_old - m_new)
        p = jnp.exp2(s - jnp.concatenate([m_new] * ncol, axis=1))
        psum = p[:, :HEAD_DIM]
        for j in range(1, ncol):
            psum = psum + p[:, j * HEAD_DIM:(j + 1) * HEAD_DIM]
        l_ref[...] = alpha * l_ref[...] + psum
        acc_ref[...] = alpha * acc_ref[...] + _dot(p.astype(BF16), v)
        m_ref[...] = m_new

    def body(c, carry):
        off = pl.multiple_of(c * tk, tk)
        step(kt_ref[:, pl.ds(off, tk)], v_ref[pl.ds(off, tk), :])
        return carry

    n_chunks = n_lat // tk

    @pl.when(pl.program_id(1) * tq < n_lat)
    def _():
        lax.fori_loop(0, n_chunks, body, 0, unroll=2 if n_chunks % 2 == 0 else 1)

    step(kt_ref[:, n_lat:n_lat + n_ctx], v_ref[n_lat:n_lat + n_ctx, :])

    o = acc_ref[...] / jnp.sum(l_ref[...], axis=1, keepdims=True)
    for h in range(GQA_REP):
        o_ref[:, h * HEAD_DIM:(h + 1) * HEAD_DIM] = o[h * tq:(h + 1) * tq].astype(o_ref.dtype)


def _global_attn(qa, ka_t, va, n_lat, n_ctx):
    n = qa.shape[0]
    tq = _tile(n_ctx, 256, 8)
    assert n_lat % tq == 0 and n_ctx % tq == 0
    tk = _tile(n_lat, 2048, 128)
    gw = GQA_REP * HEAD_DIM
    kern = functools.partial(_gattn_kernel, tq=tq, tk=tk, n_lat=n_lat, n_ctx=n_ctx)
    return pl.pallas_call(
        kern,
        grid=(N_KV_A, n // tq),
        in_specs=[
            pl.BlockSpec((tq, gw), lambda g, i: (i, g)),
            pl.BlockSpec((HEAD_DIM, n), lambda g, i: (g, 0)),
            pl.BlockSpec((n, HEAD_DIM), lambda g, i: (0, g)),
        ],
        out_specs=pl.BlockSpec((tq, gw), lambda g, i: (i, g)),
        out_shape=jax.ShapeDtypeStruct((n, W_A_Q), BF16),
        scratch_shapes=[
            pltpu.VMEM((GQA_REP * tq, HEAD_DIM), BF16),
            pltpu.VMEM((GQA_REP * tq, HEAD_DIM), F32),
            pltpu.VMEM((GQA_REP * tq, HEAD_DIM), F32),
            pltpu.VMEM((GQA_REP * tq, HEAD_DIM), F32),
        ],
        compiler_params=_params("parallel", "parallel"),
        name="global_attn",
    )(qa, ka_t, va)


def _nbr_kernel(q_ref, *rest):
    k_refs = rest[:NBR_KV_BLOCKS]
    v_refs = rest[NBR_KV_BLOCKS:2 * NBR_KV_BLOCKS]
    kc_ref, vc_ref, bias_ref, o_ref = rest[2 * NBR_KV_BLOCKS:]
    scale = HEAD_DIM ** -0.5
    nt = (((1,), (1,)), ((), ()))
    for h in range(N_HEADS_B):
        sl = slice(h * HEAD_DIM, (h + 1) * HEAD_DIM)
        q = q_ref[:, sl]
        k = jnp.concatenate([r[:, sl] for r in k_refs], axis=0)
        v = jnp.concatenate([r[:, sl] for r in v_refs], axis=0)
        s_win = lax.dot_general(q, k, nt, preferred_element_type=F32) * scale + bias_ref[h]
        s_ctx = lax.dot_general(q, kc_ref[:, sl], nt, preferred_element_type=F32) * scale
        m = jnp.maximum(jnp.max(s_win, axis=1, keepdims=True), jnp.max(s_ctx, axis=1, keepdims=True))
        p_win = jnp.exp(s_win - m)
        p_ctx = jnp.exp(s_ctx - m)
        l = jnp.sum(p_win, axis=1, keepdims=True) + jnp.sum(p_ctx, axis=1, keepdims=True)
        o = _dot(p_win.astype(BF16), v) + _dot(p_ctx.astype(BF16), vc_ref[:, sl])
        o_ref[:, sl] = (o / l).astype(o_ref.dtype)


def _nbr_bias_tables(rpb, n_lat):
    rows = n_lat // GRID_W
    nblk = n_lat // NBR_Q
    kh = min(WIN_H, rows)
    q_rows, kv_rows = NBR_Q // GRID_W, NBR_KEYS // GRID_W
    n_off_r, n_off_c = 2 * WIN_H - 1, 2 * WIN_W - 1
    blocks = (0, 1, 2, nblk - 2, nblk - 1)
    sel_r = np.zeros((len(blocks), n_off_r, q_rows, kv_rows), np.float32)
    for p, blk in enumerate(blocks):
        blk0 = min(max(blk - 2, 0), nblk - NBR_KV_BLOCKS)
        for qr in range(q_rows):
            rq = q_rows * blk + qr
            r0 = min(max(rq - kh // 2, 0), rows - kh)
            for kr in range(kv_rows):
                key_row = q_rows * blk0 + kr
                if r0 <= key_row < r0 + kh:
                    sel_r[p, key_row - rq + WIN_H - 1, qr, kr] = 1.0
    sel_c = np.zeros((n_off_c, GRID_W, GRID_W), np.float32)
    for qc in range(GRID_W):
        c0 = min(max(qc - WIN_W // 2, 0), GRID_W - WIN_W)
        for kc in range(c0, c0 + WIN_W):
            sel_c[kc - qc + WIN_W - 1, qc, kc] = 1.0
    inside = np.einsum("paqk,bcd->pqckd", sel_r, sel_c).reshape(len(blocks), NBR_Q, NBR_KEYS) > 0.5
    bias = jnp.einsum("lhab,paqk,bcd->lphqckd", rpb.astype(F32), sel_r, sel_c,
                      precision=lax.Precision.HIGHEST)
    bias = bias.reshape(rpb.shape[0], len(blocks), N_HEADS_B, NBR_Q, NBR_KEYS)
    bias = jnp.where(inside[None, :, None], bias, MASK_VALUE)
    return jnp.concatenate([bias, jnp.full_like(bias[:, :1], MASK_VALUE)], axis=1)


def _nbr_attn(qkv_b, bias, layer, n_lat, n_ctx):
    n = qkv_b.shape[0]
    nblk = n_lat // NBR_Q
    assert nblk >= NBR_KV_BLOCKS + 1 and n_ctx % NBR_Q == 0
    ctx_blk = n_lat // n_ctx
    assert n_lat % n_ctx == 0

    def first_kv(i):
        return jnp.clip(i - 2, 0, nblk - NBR_KV_BLOCKS)

    def pattern(i):
        return jnp.where(i >= nblk, NBR_PATTERNS - 1, i - first_kv(i))

    def kv_spec(col, b):
        return pl.BlockSpec((NBR_Q, W_B), lambda i: (first_kv(i) + b, col))

    in_specs = ([pl.BlockSpec((NBR_Q, W_B), lambda i: (i, 0))]
                + [kv_spec(1, b) for b in range(NBR_KV_BLOCKS)]
                + [kv_spec(2, b) for b in range(NBR_KV_BLOCKS)]
                + [pl.BlockSpec((n_ctx, W_B), lambda i: (ctx_blk, 1)),
                   pl.BlockSpec((n_ctx, W_B), lambda i: (ctx_blk, 2)),
                   pl.BlockSpec((None, None, N_HEADS_B, NBR_Q, NBR_KEYS),
                                lambda i: (layer, pattern(i), 0, 0, 0))])
    return pl.pallas_call(
        _nbr_kernel,
        grid=(n // NBR_Q,),
        in_specs=in_specs,
        out_specs=pl.BlockSpec((NBR_Q, W_B), lambda i: (i, 0)),
        out_shape=jax.ShapeDtypeStruct((n, W_B), BF16),
        compiler_params=_params("parallel"),
        name="nbr_attn",
    )(*([qkv_b] * (3 + 2 * NBR_KV_BLOCKS)), bias)


def _sgu_kernel(u_ref, vn_ref, w_ref, bt_ref, o_ref, *, n_chunks):
    for g in range(N_GROUPS_C):
        sl = slice(g * HEAD_DIM, (g + 1) * HEAD_DIM)
        w = w_ref[g]
        b = bt_ref[:, g:g + 1]
        for c in range(n_chunks):
            rs = slice(c * CHUNK, (c + 1) * CHUNK)
            sv = _dot(w, vn_ref[rs, sl]) + b
            o_ref[rs, sl] = (u_ref[rs, sl] * sv).astype(o_ref.dtype)


def _sgu(u, vn, w_s, b_t, layer):
    n = u.shape[0]
    tr = _tile(n, 1280, CHUNK)
    kern = functools.partial(_sgu_kernel, n_chunks=tr // CHUNK)
    return pl.pallas_call(
        kern,
        grid=(n // tr,),
        in_specs=[
            pl.BlockSpec((tr, W_C), lambda i: (i, 0)),
            pl.BlockSpec((tr, W_C), lambda i: (i, 0)),
            pl.BlockSpec((None, N_GROUPS_C, CHUNK, CHUNK), lambda i: (layer, 0, 0, 0)),
            pl.BlockSpec((None, CHUNK, N_GROUPS_C), lambda i: (layer, 0, 0)),
        ],
        out_specs=pl.BlockSpec((tr, W_C), lambda i: (i, 0)),
        out_shape=jax.ShapeDtypeStruct((n, W_C), BF16),
        compiler_params=_params("parallel"),
        name="sgu",
    )(u, vn, w_s, b_t)


def _out_proj_kernel(oa_ref, ob_ref, oc_ref, wa_ref, wb_ref, wc_ref, x_ref, gate_ref, o_ref, *, tm, n_lat):
    y = _dot(oa_ref[...], wa_ref[...]) + _dot(ob_ref[...], wb_ref[...]) + _dot(oc_ref[...], wc_ref[...])
    o_ref[...] = x_ref[...] + _row_select(gate_ref, tm, n_lat) * y


def _out_proj(oa, ob, oc, w_out, x, mods, layer, n_lat):
    n, d = x.shape
    tm = _tile(n, 1280, 128)
    tn = _tile(d, 512, 128)
    assert W_A_Q % W_B == 0 and W_B == W_C
    kern = functools.partial(_out_proj_kernel, tm=tm, n_lat=n_lat)
    return pl.pallas_call(
        kern,
        grid=(n // tm, d // tn),
        in_specs=[
            pl.BlockSpec((tm, W_A_Q), lambda i, j: (i, 0)),
            pl.BlockSpec((tm, W_B), lambda i, j: (i, 0)),
            pl.BlockSpec((tm, W_C), lambda i, j: (i, 0)),
            pl.BlockSpec((None, W_A_Q, tn), lambda i, j: (layer, 0, j)),
            pl.BlockSpec((None, W_B, tn), lambda i, j: (layer, W_A_Q // W_B, j)),
            pl.BlockSpec((None, W_C, tn), lambda i, j: (layer, W_A_Q // W_B + 1, j)),
            pl.BlockSpec((tm, tn), lambda i, j: (i, j)),
            pl.BlockSpec((None, None, MOD_ROWS, tn), lambda i, j: (layer, 5, 0, j)),
        ],
        out_specs=pl.BlockSpec((tm, tn), lambda i, j: (i, j)),
        out_shape=jax.ShapeDtypeStruct((n, d), F32),
        compiler_params=_params("parallel", "parallel"),
        name="out_proj",
    )(oa, ob, oc, w_out, w_out, w_out, x, mods)


def _rope_tables(n_lat, n_ctx):
    t = jnp.arange(n_lat, dtype=jnp.int32)
    row = (t // GRID_W).astype(F32)
    col = (t % GRID_W).astype(F32)
    half = HEAD_DIM // 2
    inv = ROPE_THETA ** (-jnp.arange(0, half, 2, dtype=F32) / half)
    ang_r = row[:, None] * inv[None, :]
    ang_c = col[:, None] * inv[None, :]
    ang = jnp.concatenate([ang_r, ang_r, ang_c, ang_c], axis=-1)
    lane = jnp.arange(HEAD_DIM)
    sign = jnp.where((lane % half) < HEAD_DIM // 4, -1.0, 1.0).astype(F32)
    cos = jnp.concatenate([jnp.cos(ang), jnp.ones((n_ctx, HEAD_DIM), F32)], axis=0)
    sin = jnp.concatenate([jnp.sin(ang) * sign, jnp.zeros((n_ctx, HEAD_DIM), F32)], axis=0)
    return cos, sin


def kernel(x, c, ctx, c_ctx, norm_g, mod_w_down, mod_w_up, mod_b, ffn_w_in, ffn_w_out, mix_w_in, mix_w_out,
           qk_norm_g, rpb, sgu_norm_g, sgu_w, sgu_b, final_norm_g):
    batch, n_lat, d = x.shape
    n_ctx = ctx.shape[1]
    depth = norm_g.shape[0]
    assert batch == 1 and c.shape[0] == 1

    ffn_w_in_b = ffn_w_in.astype(BF16)
    ffn_w_out_b = ffn_w_out.astype(BF16)
    mix_w_in_b = mix_w_in.astype(BF16)
    mix_w_out_b = mix_w_out.astype(BF16)
    sgu_w_b = sgu_w.astype(BF16)
    sgu_b_t = jnp.swapaxes(sgu_b, 1, 2)
    norm_g4 = norm_g.reshape(depth, 3, 1, d)

    cond = jnp.zeros((MOD_ROWS, d), F32).at[0].set(c[0]).at[1].set(c_ctx)
    mods = _adaln(cond, mod_w_down, mod_w_up, mod_b)
    mods = mods.reshape(depth, MOD_ROWS, N_MOD, d).swapaxes(1, 2)

    cos, sin = _rope_tables(n_lat, n_ctx)
    nbr_bias = _nbr_bias_tables(rpb, n_lat)
    rope_specs = [lambda tm, tn: pl.BlockSpec((tm, HEAD_DIM), lambda i, j: (i, 0))] * 2

    def g_spec(layer, which):
        return lambda tm, tn: pl.BlockSpec((None, None, 1, HEAD_DIM), lambda i, j: (layer, which, 0, 0))

    qk_g = qk_norm_g.reshape(depth, 2, 1, HEAD_DIM)
    sgu_g = sgu_norm_g.reshape(depth, 1, W_C)

    xs = jnp.concatenate([x[0], ctx[0]], axis=0)
    col_ka = W_A_Q
    col_va = col_ka + W_A_KV
    col_b = col_va + W_A_KV
    col_u = col_b + 3 * W_B
    col_v = col_u + W_C

    for l in range(depth):
        h = _norm_mod(xs, norm_g4, mods, l, 0, n_lat)
        xs = _ffn_down(_ffn_up(h, ffn_w_in_b, l, 0), ffn_w_out_b, xs, mods, l, 0, 2, n_lat)

        h = _norm_mod(xs, norm_g4, mods, l, 1, n_lat)
        qa = _proj(h, mix_w_in_b, l, 0, W_A_Q,
                   functools.partial(_epi_qk_rope, out_scale=HEAD_DIM ** -0.5 * LOG2_E),
                   [qk_g, cos, sin], [g_spec(l, 0)] + rope_specs, BF16, "proj_qa", chunk=V7X_MXU_COLS)
        ka = _proj(h, mix_w_in_b, l, col_ka, W_A_KV,
                   functools.partial(_epi_qk_rope, out_scale=1.0),
                   [qk_g, cos, sin], [g_spec(l, 1)] + rope_specs, BF16, "proj_ka", chunk=V7X_MXU_COLS)
        va = _proj(h, mix_w_in_b, l, col_va, W_A_KV, _epi_plain, [], [], BF16, "proj_va")
        qkv_b = _proj(h, mix_w_in_b, l, col_b, 3 * W_B, _epi_plain, [], [], BF16, "proj_b")
        u = _proj(h, mix_w_in_b, l, col_u, W_C, _epi_gelu, [], [], F32, "proj_u", chunk=V7X_MXU_COLS)
        vn = _proj(h, mix_w_in_b, l, col_v, W_C, _epi_gelu_rms, [sgu_g],
                   [lambda tm, tn, l=l: pl.BlockSpec((None, 1, tn), lambda i, j: (l, 0, j))], BF16, "proj_v",
                   chunk=V7X_MXU_COLS)

        oa = _global_attn(qa, ka.T, va, n_lat, n_ctx)
        ob = _nbr_attn(qkv_b, nbr_bias, l, n_lat, n_ctx)
        oc = _sgu(u, vn, sgu_w_b, sgu_b_t, l)
        xs = _out_proj(oa, ob, oc, mix_w_out_b, xs, mods, l, n_lat)

        h = _norm_mod(xs, norm_g4, mods, l, 2, n_lat)
        xs = _ffn_down(_ffn_up(h, ffn_w_in_b, l, 1), ffn_w_out_b, xs, mods, l, 1, 8, n_lat)

    return _final_norm(xs, final_norm_g, n_lat).reshape(1, n_lat, d)
```

```python
import functools
import math

import jax
import jax.numpy as jnp
import numpy as np
from jax import lax
from jax.experimental import pallas as pl
from jax.experimental.pallas import tpu as pltpu

HEAD_DIM = 128
GRID_W = 64
N_HEADS_A = 16
N_KV_A = 4
N_HEADS_B = 8
N_GROUPS_C = 8
WIN_H = 8
WIN_W = 16
CHUNK = 128
N_MOD = 9
ROPE_THETA = 10000.0
EPS = 1e-6
W_A_Q = N_HEADS_A * HEAD_DIM
W_A_KV = N_KV_A * HEAD_DIM
W_B = N_HEADS_B * HEAD_DIM
W_C = N_GROUPS_C * HEAD_DIM
GQA_REP = N_HEADS_A // N_KV_A

NBR_Q = 2 * GRID_W
NBR_KV_BLOCKS = 5
NBR_KEYS = NBR_KV_BLOCKS * NBR_Q
NBR_PATTERNS = 6
MASK_VALUE = -1e30
LOG2_E = 1.4426950408889634

V7X_VMEM_BYTES = 64 * 1024 * 1024
VMEM_LIMIT_BYTES = V7X_VMEM_BYTES - 8 * 1024 * 1024
V7X_MXU_COLS = 256
MOD_ROWS = 8
NORM_ROWS = 16

F32 = jnp.float32
BF16 = jnp.bfloat16


def _tile(n, target, align):
    best = None
    for t in range(align, min(n, target) + 1, align):
        if n % t == 0:
            best = t
    assert best is not None, (n, target, align)
    return best


def _params(*sem):
    return pltpu.CompilerParams(dimension_semantics=sem, vmem_limit_bytes=VMEM_LIMIT_BYTES)


def _row_select(mod_ref, tm, n_lat):
    row = pl.program_id(0) * tm + lax.broadcasted_iota(jnp.int32, (tm, 1), 0)
    return jnp.where(row >= n_lat, mod_ref[1:2, :], mod_ref[0:1, :])


def _dot(a, b):
    return jnp.dot(a, b, preferred_element_type=F32)


def _adaln_kernel(cond_ref, wd_ref, wu_ref, b_ref, o_ref, mid_ref):
    @pl.when(pl.program_id(1) == 0)
    def _():
        s = cond_ref[...]
        s = s * (1.0 / (1.0 + jnp.exp(-s)))
        mid_ref[...] = jnp.dot(s, wd_ref[...], preferred_element_type=F32, precision=lax.Precision.HIGHEST)

    o_ref[...] = jnp.dot(mid_ref[...], wu_ref[...], preferred_element_type=F32,
                         precision=lax.Precision.HIGHEST) + b_ref[...]


def _adaln(cond, w_down, w_up, b_up):
    depth, d, r = w_down.shape
    n = w_up.shape[-1]
    tn = _tile(n, 4096, 128)
    return pl.pallas_call(
        _adaln_kernel,
        grid=(depth, n // tn),
        in_specs=[
            pl.BlockSpec((MOD_ROWS, d), lambda l, j: (0, 0)),
            pl.BlockSpec((None, d, r), lambda l, j: (l, 0, 0)),
            pl.BlockSpec((None, r, tn), lambda l, j: (l, 0, j)),
            pl.BlockSpec((None, 1, tn), lambda l, j: (l, 0, j)),
        ],
        out_specs=pl.BlockSpec((None, MOD_ROWS, tn), lambda l, j: (l, 0, j)),
        out_shape=jax.ShapeDtypeStruct((depth, MOD_ROWS, n), F32),
        scratch_shapes=[pltpu.VMEM((MOD_ROWS, r), F32)],
        compiler_params=_params("parallel", "arbitrary"),
        name="adaln",
    )(cond, w_down, w_up, b_up.reshape(depth, 1, n))


def _norm_mod_kernel(x_ref, g_ref, shift_ref, scale_ref, o_ref, *, tr, n_lat):
    which = (pl.program_id(0) * tr >= n_lat).astype(jnp.int32)
    gain = g_ref[...] * (1.0 + scale_ref[pl.ds(which, 1), :])
    shift = shift_ref[pl.ds(which, 1), :]

    def body(r, carry):
        rs = pl.ds(pl.multiple_of(r * NORM_ROWS, NORM_ROWS), NORM_ROWS)
        x = x_ref[rs, :]
        inv = lax.rsqrt(jnp.mean(x * x, axis=-1, keepdims=True) + EPS)
        o_ref[rs, :] = (x_ref[rs, :] * inv * gain + shift).astype(o_ref.dtype)
        return carry

    lax.fori_loop(0, tr // NORM_ROWS, body, 0, unroll=2)


def _norm_mod(x, norm_g, mods, layer, sub, n_lat):
    n, d = x.shape
    tr = _tile(math.gcd(n_lat, n - n_lat), 256, NORM_ROWS)
    kern = functools.partial(_norm_mod_kernel, tr=tr, n_lat=n_lat)
    return pl.pallas_call(
        kern,
        grid=(n // tr,),
        in_specs=[
            pl.BlockSpec((tr, d), lambda i: (i, 0)),
            pl.BlockSpec((None, None, 1, d), lambda i: (layer, sub, 0, 0)),
            pl.BlockSpec((None, None, MOD_ROWS, d), lambda i: (layer, 3 * sub, 0, 0)),
            pl.BlockSpec((None, None, MOD_ROWS, d), lambda i: (layer, 3 * sub + 1, 0, 0)),
        ],
        out_specs=pl.BlockSpec((tr, d), lambda i: (i, 0)),
        out_shape=jax.ShapeDtypeStruct((n, d), BF16),
        compiler_params=_params("parallel"),
        name="norm_mod",
    )(x, norm_g, mods, mods)


def _final_norm_kernel(x_ref, g_ref, o_ref):
    g = g_ref[...]

    def body(r, carry):
        rs = pl.ds(pl.multiple_of(r * NORM_ROWS, NORM_ROWS), NORM_ROWS)
        x = x_ref[rs, :]
        inv = lax.rsqrt(jnp.mean(x * x, axis=-1, keepdims=True) + EPS)
        o_ref[rs, :] = x_ref[rs, :] * inv * g
        return carry

    lax.fori_loop(0, x_ref.shape[0] // NORM_ROWS, body, 0, unroll=2)


def _final_norm(x, g, n_lat):
    d = x.shape[1]
    tr = _tile(n_lat, 256, NORM_ROWS)
    return pl.pallas_call(
        _final_norm_kernel,
        grid=(n_lat // tr,),
        in_specs=[pl.BlockSpec((tr, d), lambda i: (i, 0)),
                  pl.BlockSpec((1, d), lambda i: (0, 0))],
        out_specs=pl.BlockSpec((tr, d), lambda i: (i, 0)),
        out_shape=jax.ShapeDtypeStruct((n_lat, d), F32),
        compiler_params=_params("parallel"),
        name="final_norm",
    )(x, g.reshape(1, d))


def _ffn_up_kernel(h_ref, wa_ref, wb_ref, o_ref):
    h = h_ref[...]
    a = _dot(h, wa_ref[...])
    b = _dot(h, wb_ref[...])
    o_ref[...] = (a * (1.0 / (1.0 + jnp.exp(-a))) * b).astype(o_ref.dtype)


def _ffn_up(h, w_in, layer, half):
    n, d = h.shape
    f = w_in.shape[-1] // 2
    tm = _tile(n, 640, 128)
    tn = _tile(f, 512, 128)
    nb = f // tn
    return pl.pallas_call(
        _ffn_up_kernel,
        grid=(n // tm, nb),
        in_specs=[
            pl.BlockSpec((tm, d), lambda i, j: (i, 0)),
            pl.BlockSpec((None, None, d, tn), lambda i, j: (layer, half, 0, j)),
            pl.BlockSpec((None, None, d, tn), lambda i, j: (layer, half, 0, j + nb)),
        ],
        out_specs=pl.BlockSpec((tm, tn), lambda i, j: (i, j)),
        out_shape=jax.ShapeDtypeStruct((n, f), BF16),
        compiler_params=_params("parallel", "parallel"),
        name="ffn_up",
    )(h, w_in, w_in)


def _ffn_down_kernel(g_ref, w_ref, x_ref, gate_ref, o_ref, *, tm, n_lat):
    y = _dot(g_ref[...], w_ref[...])
    o_ref[...] = x_ref[...] + 0.5 * _row_select(gate_ref, tm, n_lat) * y


def _ffn_down(g, w_out, x, mods, layer, half, gate_idx, n_lat):
    n, f = g.shape
    d = x.shape[1]
    tm = _tile(n, 640, 128)
    tn = _tile(d, 512, 128)
    kern = functools.partial(_ffn_down_kernel, tm=tm, n_lat=n_lat)
    return pl.pallas_call(
        kern,
        grid=(n // tm, d // tn),
        in_specs=[
            pl.BlockSpec((tm, f), lambda i, j: (i, 0)),
            pl.BlockSpec((None, None, f, tn), lambda i, j: (layer, half, 0, j)),
            pl.BlockSpec((tm, tn), lambda i, j: (i, j)),
            pl.BlockSpec((None, None, MOD_ROWS, tn), lambda i, j: (layer, gate_idx, 0, j)),
        ],
        out_specs=pl.BlockSpec((tm, tn), lambda i, j: (i, j)),
        out_shape=jax.ShapeDtypeStruct((n, d), F32),
        compiler_params=_params("parallel", "parallel"),
        name="ffn_down",
    )(g, w_out, x, mods)


def _head_rms(x, g):
    return x * lax.rsqrt(jnp.mean(x * x, axis=-1, keepdims=True) + EPS) * g


def _rope(x, cos, sin_signed):
    lane = lax.broadcasted_iota(jnp.int32, x.shape, 1)
    low = jnp.bitwise_and(lane, HEAD_DIM // 2 - 1) < (HEAD_DIM // 4)
    rot = jnp.where(low, pltpu.roll(x, HEAD_DIM - HEAD_DIM // 4, 1), pltpu.roll(x, HEAD_DIM // 4, 1))
    return x * cos + rot * sin_signed


def _epi_qk_rope(acc, cols, g_ref, cos_ref, sin_ref, *, out_scale):
    g = g_ref[...]
    cos = cos_ref[...]
    sin = sin_ref[...]
    heads = []
    for h in range(acc.shape[1] // HEAD_DIM):
        x = _head_rms(acc[:, h * HEAD_DIM:(h + 1) * HEAD_DIM], g)
        heads.append(_rope(x, cos, sin) * out_scale)
    return jnp.concatenate(heads, axis=1)


def _epi_plain(acc, cols):
    return acc


def _epi_qkv_b(acc, cols):
    return acc * jnp.where(pl.program_id(1) == 0, HEAD_DIM ** -0.5 * LOG2_E, 1.0)


def _epi_kv_a(acc, cols, g_ref, cos_ref, sin_ref):
    if cols.start < W_A_KV:
        return _epi_qk_rope(acc, cols, g_ref, cos_ref, sin_ref, out_scale=1.0)
    return acc


def _gelu_tanh(x):
    return 0.5 * x * (1.0 + jnp.tanh(0.7978845608028654 * (x + 0.044715 * (x * x * x))))


def _epi_gelu(acc, cols):
    return _gelu_tanh(acc)


def _epi_gelu_rms(acc, cols, g_ref):
    y = _gelu_tanh(acc)
    g = g_ref[:, cols]
    groups = []
    for c in range(acc.shape[1] // HEAD_DIM):
        sl = slice(c * HEAD_DIM, (c + 1) * HEAD_DIM)
        groups.append(_head_rms(y[:, sl], g[:, sl]))
    return jnp.concatenate(groups, axis=1)


def _proj_kernel(h_ref, w_ref, *rest, epilogue, chunk):
    extras, o_ref = rest[:-1], rest[-1]
    h = h_ref[...]
    for c in range(o_ref.shape[1] // chunk):
        cols = slice(c * chunk, (c + 1) * chunk)
        o_ref[:, cols] = epilogue(_dot(h, w_ref[:, cols]), cols, *extras).astype(o_ref.dtype)


def _proj(h, w_in, layer, col_off, n_cols, epilogue, extras, extra_specs, out_dtype, name, chunk=None, tn=None):
    n, d = h.shape
    tm = _tile(n, 640, 128)
    tn = _tile(n_cols, 1024, 128) if tn is None else tn
    chunk = tn if chunk is None else chunk
    assert col_off % tn == 0 and tn % chunk == 0
    off = col_off // tn
    return pl.pallas_call(
        functools.partial(_proj_kernel, epilogue=epilogue, chunk=chunk),
        grid=(n // tm, n_cols // tn),
        in_specs=[
            pl.BlockSpec((tm, d), lambda i, j: (i, 0)),
            pl.BlockSpec((None, d, tn), lambda i, j: (layer, 0, j + off)),
        ] + [spec(tm, tn) for spec in extra_specs],
        out_specs=pl.BlockSpec((tm, tn), lambda i, j: (i, j)),
        out_shape=jax.ShapeDtypeStruct((n, n_cols), out_dtype),
        compiler_params=_params("parallel", "parallel"),
        name=name,
    )(h, w_in, *extras)


def _gattn_kernel(q_ref, kt_ref, v_ref, o_ref, qs_ref, m_ref, l_ref, acc_ref, *, tq, tk, n_lat, n_ctx):
    for h in range(GQA_REP):
        qs_ref[h * tq:(h + 1) * tq, :] = q_ref[:, h * HEAD_DIM:(h + 1) * HEAD_DIM]
    m_ref[...] = jnp.full(m_ref.shape, MASK_VALUE, F32)
    l_ref[...] = jnp.zeros(l_ref.shape, F32)
    acc_ref[...] = jnp.zeros(acc_ref.shape, F32)

    def step(kt, v):
        ncol = v.shape[0] // HEAD_DIM
        s = _dot(qs_ref[...], kt)
        m_old = m_ref[...]
        m_new = jnp.maximum(m_old, jnp.max(s, axis=1, keepdims=True))
        alpha = jnp.exp2(m_old - m_new)
        p = jnp.exp2(s - jnp.concatenate([m_new] * ncol, axis=1))
        psum = p[:, :HEAD_DIM]
        for j in range(1, ncol):
            psum = psum + p[:, j * HEAD_DIM:(j + 1) * HEAD_DIM]
        l_ref[...] = alpha * l_ref[...] + psum
        acc_ref[...] = alpha * acc_ref[...] + _dot(p.astype(BF16), v)
        m_ref[...] = m_new

    def body(c, carry):
        off = pl.multiple_of(c * tk, tk)
        step(kt_ref[:, pl.ds(off, tk)], v_ref[pl.ds(off, tk), :])
        return carry

    n_chunks = n_lat // tk

    @pl.when(pl.program_id(1) * tq < n_lat)
    def _():
        lax.fori_loop(0, n_chunks, body, 0, unroll=2 if n_chunks % 2 == 0 else 1)

    step(kt_ref[:, n_lat:n_lat + n_ctx], v_ref[n_lat:n_lat + n_ctx, :])

    o = acc_ref[...] / jnp.sum(l_ref[...], axis=1, keepdims=True)
    for h in range(GQA_REP):
        o_ref[:, h * HEAD_DIM:(h + 1) * HEAD_DIM] = o[h * tq:(h + 1) * tq].astype(o_ref.dtype)


def _global_attn(qa, ka_t, kv_a, n_lat, n_ctx):
    n = qa.shape[0]
    tq = _tile(n_ctx, 256, 8)
    assert n_lat % tq == 0 and n_ctx % tq == 0
    tk = _tile(n_lat, 2048, 128)
    gw = GQA_REP * HEAD_DIM
    kern = functools.partial(_gattn_kernel, tq=tq, tk=tk, n_lat=n_lat, n_ctx=n_ctx)
    return pl.pallas_call(
        kern,
        grid=(N_KV_A, n // tq),
        in_specs=[
            pl.BlockSpec((tq, gw), lambda g, i: (i, g)),
            pl.BlockSpec((HEAD_DIM, n), lambda g, i: (g, 0)),
            pl.BlockSpec((n, HEAD_DIM), lambda g, i: (0, N_KV_A + g)),
        ],
        out_specs=pl.BlockSpec((tq, gw), lambda g, i: (i, g)),
        out_shape=jax.ShapeDtypeStruct((n, W_A_Q), BF16),
        scratch_shapes=[
            pltpu.VMEM((GQA_REP * tq, HEAD_DIM), BF16),
            pltpu.VMEM((GQA_REP * tq, HEAD_DIM), F32),
            pltpu.VMEM((GQA_REP * tq, HEAD_DIM), F32),
            pltpu.VMEM((GQA_REP * tq, HEAD_DIM), F32),
        ],
        compiler_params=_params("parallel", "parallel"),
        name="global_attn",
    )(qa, ka_t, kv_a)


def _nbr_kernel(q_ref, *rest):
    k_refs = rest[:NBR_KV_BLOCKS]
    v_refs = rest[NBR_KV_BLOCKS:2 * NBR_KV_BLOCKS]
    kc_ref, vc_ref, bias_ref, o_ref = rest[2 * NBR_KV_BLOCKS:]
    nt = (((1,), (1,)), ((), ()))
    heads = [slice(h * HEAD_DIM, (h + 1) * HEAD_DIM) for h in range(N_HEADS_B)]
    scores = []
    for h, sl in enumerate(heads):
        k = jnp.concatenate([r[:, sl] for r in k_refs] + [kc_ref[:, sl]], axis=0)
        scores.append(lax.dot_general(q_ref[:, sl], k, nt, preferred_element_type=F32) + bias_ref[h])
    probs, sums = [], []
    for s in scores:
        p = jnp.exp2(s - jnp.max(s, axis=1, keepdims=True))
        sums.append(jnp.sum(p, axis=1, keepdims=True))
        probs.append(p.astype(BF16))
    for h, sl in enumerate(heads):
        v = jnp.concatenate([r[:, sl] for r in v_refs] + [vc_ref[:, sl]], axis=0)
        o_ref[:, sl] = (_dot(probs[h], v) / sums[h]).astype(o_ref.dtype)


def _nbr_bias_tables(rpb, n_lat, n_ctx):
    rows = n_lat // GRID_W
    nblk = n_lat // NBR_Q
    kh = min(WIN_H, rows)
    q_rows, kv_rows = NBR_Q // GRID_W, NBR_KEYS // GRID_W
    n_off_r, n_off_c = 2 * WIN_H - 1, 2 * WIN_W - 1
    blocks = (0, 1, 2, nblk - 2, nblk - 1)
    sel_r = np.zeros((len(blocks), n_off_r, q_rows, kv_rows), np.float32)
    for p, blk in enumerate(blocks):
        blk0 = min(max(blk - 2, 0), nblk - NBR_KV_BLOCKS)
        for qr in range(q_rows):
            rq = q_rows * blk + qr
            r0 = min(max(rq - kh // 2, 0), rows - kh)
            for kr in range(kv_rows):
                key_row = q_rows * blk0 + kr
                if r0 <= key_row < r0 + kh:
                    sel_r[p, key_row - rq + WIN_H - 1, qr, kr] = 1.0
    sel_c = np.zeros((n_off_c, GRID_W, GRID_W), np.float32)
    for qc in range(GRID_W):
        c0 = min(max(qc - WIN_W // 2, 0), GRID_W - WIN_W)
        for kc in range(c0, c0 + WIN_W):
            sel_c[kc - qc + WIN_W - 1, qc, kc] = 1.0
    inside = np.einsum("paqk,bcd->pqckd", sel_r, sel_c).reshape(len(blocks), NBR_Q, NBR_KEYS) > 0.5
    bias = jnp.einsum("lhab,paqk,bcd->lphqckd", rpb.astype(F32), sel_r, sel_c,
                      precision=lax.Precision.HIGHEST)
    bias = bias.reshape(rpb.shape[0], len(blocks), N_HEADS_B, NBR_Q, NBR_KEYS) * LOG2_E
    bias = jnp.where(inside[None, :, None], bias, MASK_VALUE)
    bias = jnp.concatenate([bias, jnp.full_like(bias[:, :1], MASK_VALUE)], axis=1)
    return jnp.pad(bias, ((0, 0),) * 4 + ((0, n_ctx),))


def _nbr_attn(qkv_b, bias, layer, n_lat, n_ctx):
    n = qkv_b.shape[0]
    nblk = n_lat // NBR_Q
    assert nblk >= NBR_KV_BLOCKS + 1 and n_ctx % NBR_Q == 0
    ctx_blk = n_lat // n_ctx
    assert n_lat % n_ctx == 0

    def first_kv(i):
        return jnp.clip(i - 2, 0, nblk - NBR_KV_BLOCKS)

    def pattern(i):
        return jnp.where(i >= nblk, NBR_PATTERNS - 1, i - first_kv(i))

    def kv_spec(col, b):
        return pl.BlockSpec((NBR_Q, W_B), lambda i: (first_kv(i) + b, col))

    in_specs = ([pl.BlockSpec((NBR_Q, W_B), lambda i: (i, 0))]
                + [kv_spec(1, b) for b in range(NBR_KV_BLOCKS)]
                + [kv_spec(2, b) for b in range(NBR_KV_BLOCKS)]
                + [pl.BlockSpec((n_ctx, W_B), lambda i: (ctx_blk, 1)),
                   pl.BlockSpec((n_ctx, W_B), lambda i: (ctx_blk, 2)),
                   pl.BlockSpec((None, None, N_HEADS_B, NBR_Q, NBR_KEYS + n_ctx),
                                lambda i: (layer, pattern(i), 0, 0, 0))])
    return pl.pallas_call(
        _nbr_kernel,
        grid=(n // NBR_Q,),
        in_specs=in_specs,
        out_specs=pl.BlockSpec((NBR_Q, W_B), lambda i: (i, 0)),
        out_shape=jax.ShapeDtypeStruct((n, W_B), BF16),
        compiler_params=_params("parallel"),
        name="nbr_attn",
    )(*([qkv_b] * (3 + 2 * NBR_KV_BLOCKS)), bias)


def _sgu_kernel(u_ref, vn_ref, w_ref, bt_ref, o_ref, *, n_chunks):
    for g in range(N_GROUPS_C):
        sl = slice(g * HEAD_DIM, (g + 1) * HEAD_DIM)
        w = w_ref[g]
        b = bt_ref[:, g:g + 1]
        for c in range(n_chunks):
            rs = slice(c * CHUNK, (c + 1) * CHUNK)
            sv = _dot(w, vn_ref[rs, sl]) + b
            o_ref[rs, sl] = (u_ref[rs, sl] * sv).astype(o_ref.dtype)


def _sgu(u, vn, w_s, b_t, layer):
    n = u.shape[0]
    tr = _tile(n, 1280, CHUNK)
    kern = functools.partial(_sgu_kernel, n_chunks=tr // CHUNK)
    return pl.pallas_call(
        kern,
        grid=(n // tr,),
        in_specs=[
            pl.BlockSpec((tr, W_C), lambda i: (i, 0)),
            pl.BlockSpec((tr, W_C), lambda i: (i, 0)),
            pl.BlockSpec((None, N_GROUPS_C, CHUNK, CHUNK), lambda i: (layer, 0, 0, 0)),
            pl.BlockSpec((None, CHUNK, N_GROUPS_C), lambda i: (layer, 0, 0)),
        ],
        out_specs=pl.BlockSpec((tr, W_C), lambda i: (i, 0)),
        out_shape=jax.ShapeDtypeStruct((n, W_C), BF16),
        compiler_params=_params("parallel"),
        name="sgu",
    )(u, vn, w_s, b_t)


def _out_proj_kernel(oa_ref, ob_ref, oc_ref, wa_ref, wb_ref, wc_ref, x_ref, gate_ref, o_ref, *, tm, n_lat):
    y = _dot(oa_ref[...], wa_ref[...]) + _dot(ob_ref[...], wb_ref[...]) + _dot(oc_ref[...], wc_ref[...])
    o_ref[...] = x_ref[...] + _row_select(gate_ref, tm, n_lat) * y


def _out_proj(oa, ob, oc, w_out, x, mods, layer, n_lat):
    n, d = x.shape
    tm = _tile(n, 1280, 128)
    tn = _tile(d, 512, 128)
    assert W_A_Q % W_B == 0 and W_B == W_C
    kern = functools.partial(_out_proj_kernel, tm=tm, n_lat=n_lat)
    return pl.pallas_call(
        kern,
        grid=(n // tm, d // tn),
        in_specs=[
            pl.BlockSpec((tm, W_A_Q), lambda i, j: (i, 0)),
            pl.BlockSpec((tm, W_B), lambda i, j: (i, 0)),
            pl.BlockSpec((tm, W_C), lambda i, j: (i, 0)),
            pl.BlockSpec((None, W_A_Q, tn), lambda i, j: (layer, 0, j)),
            pl.BlockSpec((None, W_B, tn), lambda i, j: (layer, W_A_Q // W_B, j)),
            pl.BlockSpec((None, W_C, tn), lambda i, j: (layer, W_A_Q // W_B + 1, j)),
            pl.BlockSpec((tm, tn), lambda i, j: (i, j)),
            pl.BlockSpec((None, None, MOD_ROWS, tn), lambda i, j: (layer, 5, 0, j)),
        ],
        out_specs=pl.BlockSpec((tm, tn), lambda i, j: (i, j)),
        out_shape=jax.ShapeDtypeStruct((n, d), F32),
        compiler_params=_params("parallel", "parallel"),
        name="out_proj",
    )(oa, ob, oc, w_out, w_out, w_out, x, mods)


def _rope_tables(n_lat, n_ctx):
    t = jnp.arange(n_lat, dtype=jnp.int32)
    row = (t // GRID_W).astype(F32)
    col = (t % GRID_W).astype(F32)
    half = HEAD_DIM // 2
    inv = ROPE_THETA ** (-jnp.arange(0, half, 2, dtype=F32) / half)
    ang_r = row[:, None] * inv[None, :]
    ang_c = col[:, None] * inv[None, :]
    ang = jnp.concatenate([ang_r, ang_r, ang_c, ang_c], axis=-1)
    lane = jnp.arange(HEAD_DIM)
    sign = jnp.where((lane % half) < HEAD_DIM // 4, -1.0, 1.0).astype(F32)
    cos = jnp.concatenate([jnp.cos(ang), jnp.ones((n_ctx, HEAD_DIM), F32)], axis=0)
    sin = jnp.concatenate([jnp.sin(ang) * sign, jnp.zeros((n_ctx, HEAD_DIM), F32)], axis=0)
    return cos, sin


def _mixer_groups(h, l, mix_w_in_b, qk_g, sgu_g, cos, sin, nbr_bias, sgu_w_b, sgu_b_t, n_lat, n_ctx):
    rope_specs = [lambda tm, tn: pl.BlockSpec((tm, HEAD_DIM), lambda i, j: (i, 0))] * 2

    def g_spec(which):
        return lambda tm, tn: pl.BlockSpec((None, None, 1, HEAD_DIM), lambda i, j: (l, which, 0, 0))

    col_ka = W_A_Q
    col_b = col_ka + 2 * W_A_KV
    col_u = col_b + 3 * W_B
    col_v = col_u + W_C
    qa = _proj(h, mix_w_in_b, l, 0, W_A_Q,
               functools.partial(_epi_qk_rope, out_scale=HEAD_DIM ** -0.5 * LOG2_E),
               [qk_g, cos, sin], [g_spec(0)] + rope_specs, BF16, "proj_qa", chunk=V7X_MXU_COLS)
    kv_a = _proj(h, mix_w_in_b, l, col_ka, 2 * W_A_KV, _epi_kv_a,
                 [qk_g, cos, sin], [g_spec(1)] + rope_specs, BF16, "proj_kv_a", chunk=V7X_MXU_COLS)
    qkv_b = _proj(h, mix_w_in_b, l, col_b, 3 * W_B, _epi_qkv_b, [], [], BF16, "proj_b", tn=W_B)
    u = _proj(h, mix_w_in_b, l, col_u, W_C, _epi_gelu, [], [], F32, "proj_u", chunk=V7X_MXU_COLS)
    vn = _proj(h, mix_w_in_b, l, col_v, W_C, _epi_gelu_rms, [sgu_g],
               [lambda tm, tn: pl.BlockSpec((None, 1, tn), lambda i, j: (l, 0, j))], BF16, "proj_v",
               chunk=V7X_MXU_COLS)
    oa = _global_attn(qa, kv_a[:, :W_A_KV].T, kv_a, n_lat, n_ctx)
    ob = _nbr_attn(qkv_b, nbr_bias, l, n_lat, n_ctx)
    oc = _sgu(u, vn, sgu_w_b, sgu_b_t, l)
    return oa, ob, oc


def kernel(x, c, ctx, c_ctx, norm_g, mod_w_down, mod_w_up, mod_b, ffn_w_in, ffn_w_out, mix_w_in, mix_w_out,
           qk_norm_g, rpb, sgu_norm_g, sgu_w, sgu_b, final_norm_g):
    batch, n_lat, d = x.shape
    n_ctx = ctx.shape[1]
    depth = norm_g.shape[0]
    assert batch == 1 and c.shape[0] == 1

    ffn_w_in_b = ffn_w_in.astype(BF16)
    ffn_w_out_b = ffn_w_out.astype(BF16)
    mix_w_in_b = mix_w_in.astype(BF16)
    mix_w_out_b = mix_w_out.astype(BF16)
    sgu_w_b = sgu_w.astype(BF16)
    sgu_b_t = jnp.swapaxes(sgu_b, 1, 2)
    norm_g4 = norm_g.reshape(depth, 3, 1, d)

    cond = jnp.zeros((MOD_ROWS, d), F32).at[0].set(c[0]).at[1].set(c_ctx)
    mods = _adaln(cond, mod_w_down, mod_w_up, mod_b)
    mods = mods.reshape(depth, MOD_ROWS, N_MOD, d).swapaxes(1, 2)

    cos, sin = _rope_tables(n_lat, n_ctx)
    nbr_bias = _nbr_bias_tables(rpb, n_lat, n_ctx)
    qk_g = qk_norm_g.reshape(depth, 2, 1, HEAD_DIM)
    sgu_g = sgu_norm_g.reshape(depth, 1, W_C)

    xs = jnp.concatenate([x[0], ctx[0]], axis=0)

    for l in range(depth):
        h = _norm_mod(xs, norm_g4, mods, l, 0, n_lat)
        xs = _ffn_down(_ffn_up(h, ffn_w_in_b, l, 0), ffn_w_out_b, xs, mods, l, 0, 2, n_lat)

        h = _norm_mod(xs, norm_g4, mods, l, 1, n_lat)
        oa, ob, oc = _mixer_groups(h, l, mix_w_in_b, qk_g, sgu_g, cos, sin, nbr_bias, sgu_w_b, sgu_b_t,
                                   n_lat, n_ctx)
        xs = _out_proj(oa, ob, oc, mix_w_out_b, xs, mods, l, n_lat)

        h = _norm_mod(xs, norm_g4, mods, l, 2, n_lat)
        xs = _ffn_down(_ffn_up(h, ffn_w_in_b, l, 1), ffn_w_out_b, xs, mods, l, 1, 8, n_lat)

    return _final_norm(xs, final_norm_g, n_lat).reshape(1, n_lat, d)
```

```python
import functools
import math

import jax
import jax.numpy as jnp
import numpy as np
from jax import lax
from jax.experimental import pallas as pl
from jax.experimental.pallas import tpu as pltpu

HEAD_DIM = 128
GRID_W = 64
N_HEADS_A = 16
N_KV_A = 4
N_HEADS_B = 8
N_GROUPS_C = 8
WIN_H = 8
WIN_W = 16
CHUNK = 128
N_MOD = 9
ROPE_THETA = 10000.0
EPS = 1e-6
W_A_Q = N_HEADS_A * HEAD_DIM
W_A_KV = N_KV_A * HEAD_DIM
W_B = N_HEADS_B * HEAD_DIM
W_C = N_GROUPS_C * HEAD_DIM
GQA_REP = N_HEADS_A // N_KV_A

NBR_Q = 2 * GRID_W
NBR_KV_BLOCKS = 5
NBR_KEYS = NBR_KV_BLOCKS * NBR_Q
NBR_PATTERNS = 6
MASK_VALUE = -1e30
LOG2_E = 1.4426950408889634

V7X_VMEM_BYTES = 64 * 1024 * 1024
VMEM_LIMIT_BYTES = V7X_VMEM_BYTES - 8 * 1024 * 1024
V7X_MXU_COLS = 256
GATTN_KV_CHUNK = 3584
GATTN_MAX_UNROLL = 8
MOD_ROWS = 8
NORM_ROWS = 16

F32 = jnp.float32
BF16 = jnp.bfloat16


def _tile(n, target, align):
    best = None
    for t in range(align, min(n, target) + 1, align):
        if n % t == 0:
            best = t
    assert best is not None, (n, target, align)
    return best


def _params(*sem):
    return pltpu.CompilerParams(dimension_semantics=sem, vmem_limit_bytes=VMEM_LIMIT_BYTES)


def _row_select(mod_ref, tm, n_lat):
    row = pl.program_id(0) * tm + lax.broadcasted_iota(jnp.int32, (tm, 1), 0)
    return jnp.where(row >= n_lat, mod_ref[1:2, :], mod_ref[0:1, :])


def _dot(a, b):
    return jnp.dot(a, b, preferred_element_type=F32)


def _adaln_kernel(cond_ref, wd_ref, wu_ref, b_ref, o_ref, mid_ref):
    @pl.when(pl.program_id(1) == 0)
    def _():
        s = cond_ref[...]
        s = s * (1.0 / (1.0 + jnp.exp(-s)))
        mid_ref[...] = jnp.dot(s, wd_ref[...], preferred_element_type=F32, precision=lax.Precision.HIGHEST)

    o_ref[...] = jnp.dot(mid_ref[...], wu_ref[...], preferred_element_type=F32,
                         precision=lax.Precision.HIGHEST) + b_ref[...]


def _adaln(cond, w_down, w_up, b_up):
    depth, d, r = w_down.shape
    n = w_up.shape[-1]
    tn = _tile(n, 4096, 128)
    return pl.pallas_call(
        _adaln_kernel,
        grid=(depth, n // tn),
        in_specs=[
            pl.BlockSpec((MOD_ROWS, d), lambda l, j: (0, 0)),
            pl.BlockSpec((None, d, r), lambda l, j: (l, 0, 0)),
            pl.BlockSpec((None, r, tn), lambda l, j: (l, 0, j)),
            pl.BlockSpec((None, 1, tn), lambda l, j: (l, 0, j)),
        ],
        out_specs=pl.BlockSpec((None, MOD_ROWS, tn), lambda l, j: (l, 0, j)),
        out_shape=jax.ShapeDtypeStruct((depth, MOD_ROWS, n), F32),
        scratch_shapes=[pltpu.VMEM((MOD_ROWS, r), F32)],
        compiler_params=_params("parallel", "arbitrary"),
        name="adaln",
    )(cond, w_down, w_up, b_up.reshape(depth, 1, n))


def _norm_mod_kernel(x_ref, g_ref, shift_ref, scale_ref, o_ref, *, tr, n_lat):
    which = (pl.program_id(0) * tr >= n_lat).astype(jnp.int32)
    gain = g_ref[...] * (1.0 + scale_ref[pl.ds(which, 1), :])
    shift = shift_ref[pl.ds(which, 1), :]

    def body(r, carry):
        rs = pl.ds(pl.multiple_of(r * NORM_ROWS, NORM_ROWS), NORM_ROWS)
        x = x_ref[rs, :]
        inv = lax.rsqrt(jnp.mean(x * x, axis=-1, keepdims=True) + EPS)
        o_ref[rs, :] = (x_ref[rs, :] * inv * gain + shift).astype(o_ref.dtype)
        return carry

    lax.fori_loop(0, tr // NORM_ROWS, body, 0, unroll=2)


def _norm_mod(x, norm_g, mods, layer, sub, n_lat):
    n, d = x.shape
    tr = _tile(math.gcd(n_lat, n - n_lat), 256, NORM_ROWS)
    kern = functools.partial(_norm_mod_kernel, tr=tr, n_lat=n_lat)
    return pl.pallas_call(
        kern,
        grid=(n // tr,),
        in_specs=[
            pl.BlockSpec((tr, d), lambda i: (i, 0)),
            pl.BlockSpec((None, None, 1, d), lambda i: (layer, sub, 0, 0)),
            pl.BlockSpec((None, None, MOD_ROWS, d), lambda i: (layer, 3 * sub, 0, 0)),
            pl.BlockSpec((None, None, MOD_ROWS, d), lambda i: (layer, 3 * sub + 1, 0, 0)),
        ],
        out_specs=pl.BlockSpec((tr, d), lambda i: (i, 0)),
        out_shape=jax.ShapeDtypeStruct((n, d), BF16),
        compiler_params=_params("parallel"),
        name="norm_mod",
    )(x, norm_g, mods, mods)


def _final_norm_kernel(x_ref, g_ref, o_ref):
    g = g_ref[...]

    def body(r, carry):
        rs = pl.ds(pl.multiple_of(r * NORM_ROWS, NORM_ROWS), NORM_ROWS)
        x = x_ref[rs, :]
        inv = lax.rsqrt(jnp.mean(x * x, axis=-1, keepdims=True) + EPS)
        o_ref[rs, :] = x_ref[rs, :] * inv * g
        return carry

    lax.fori_loop(0, x_ref.shape[0] // NORM_ROWS, body, 0, unroll=2)


def _final_norm(x, g, n_lat):
    d = x.shape[1]
    tr = _tile(n_lat, 256, NORM_ROWS)
    return pl.pallas_call(
        _final_norm_kernel,
        grid=(n_lat // tr,),
        in_specs=[pl.BlockSpec((tr, d), lambda i: (i, 0)),
                  pl.BlockSpec((1, d), lambda i: (0, 0))],
        out_specs=pl.BlockSpec((tr, d), lambda i: (i, 0)),
        out_shape=jax.ShapeDtypeStruct((n_lat, d), F32),
        compiler_params=_params("parallel"),
        name="final_norm",
    )(x, g.reshape(1, d))


def _ffn_up_kernel(h_ref, wa_ref, wb_ref, o_ref):
    h = h_ref[...]
    a = _dot(h, wa_ref[...])
    b = _dot(h, wb_ref[...])
    o_ref[...] = (a * (1.0 / (1.0 + jnp.exp(-a))) * b).astype(o_ref.dtype)


def _ffn_up(h, w_in, layer, half):
    n, d = h.shape
    f = w_in.shape[-1] // 2
    tm = _tile(n, 640, 128)
    tn = _tile(f, 512, 128)
    nb = f // tn
    return pl.pallas_call(
        _ffn_up_kernel,
        grid=(n // tm, nb),
        in_specs=[
            pl.BlockSpec((tm, d), lambda i, j: (i, 0)),
            pl.BlockSpec((None, None, d, tn), lambda i, j: (layer, half, 0, j)),
            pl.BlockSpec((None, None, d, tn), lambda i, j: (layer, half, 0, j + nb)),
        ],
        out_specs=pl.BlockSpec((tm, tn), lambda i, j: (i, j)),
        out_shape=jax.ShapeDtypeStruct((n, f), BF16),
        compiler_params=_params("parallel", "parallel"),
        name="ffn_up",
    )(h, w_in, w_in)


def _ffn_down_kernel(g_ref, w_ref, x_ref, gate_ref, o_ref, *, tm, n_lat):
    y = _dot(g_ref[...], w_ref[...])
    o_ref[...] = x_ref[...] + 0.5 * _row_select(gate_ref, tm, n_lat) * y


def _ffn_down(g, w_out, x, mods, layer, half, gate_idx, n_lat):
    n, f = g.shape
    d = x.shape[1]
    tm = _tile(n, 640, 128)
    tn = _tile(d, 512, 128)
    kern = functools.partial(_ffn_down_kernel, tm=tm, n_lat=n_lat)
    return pl.pallas_call(
        kern,
        grid=(n // tm, d // tn),
        in_specs=[
            pl.BlockSpec((tm, f), lambda i, j: (i, 0)),
            pl.BlockSpec((None, None, f, tn), lambda i, j: (layer, half, 0, j)),
            pl.BlockSpec((tm, tn), lambda i, j: (i, j)),
            pl.BlockSpec((None, None, MOD_ROWS, tn), lambda i, j: (layer, gate_idx, 0, j)),
        ],
        out_specs=pl.BlockSpec((tm, tn), lambda i, j: (i, j)),
        out_shape=jax.ShapeDtypeStruct((n, d), F32),
        compiler_params=_params("parallel", "parallel"),
        name="ffn_down",
    )(g, w_out, x, mods)


def _head_rms(x, g):
    return x * lax.rsqrt(jnp.mean(x * x, axis=-1, keepdims=True) + EPS) * g


def _rope(x, cos, sin_signed):
    lane = lax.broadcasted_iota(jnp.int32, x.shape, 1)
    low = jnp.bitwise_and(lane, HEAD_DIM // 2 - 1) < (HEAD_DIM // 4)
    rot = jnp.where(low, pltpu.roll(x, HEAD_DIM - HEAD_DIM // 4, 1), pltpu.roll(x, HEAD_DIM // 4, 1))
    return x * cos + rot * sin_signed


def _epi_qk_rope(acc, cols, g_ref, cos_ref, sin_ref, *, out_scale):
    g = g_ref[...]
    cos = cos_ref[...]
    sin = sin_ref[...]
    heads = []
    for h in range(acc.shape[1] // HEAD_DIM):
        x = _head_rms(acc[:, h * HEAD_DIM:(h + 1) * HEAD_DIM], g)
        heads.append(_rope(x, cos, sin) * out_scale)
    return jnp.concatenate(heads, axis=1)


def _epi_plain(acc, cols):
    return acc


def _epi_qkv_b(acc, cols):
    return acc * jnp.where(pl.program_id(1) == 0, HEAD_DIM ** -0.5 * LOG2_E, 1.0)


def _epi_kv_a(acc, cols, g_ref, cos_ref, sin_ref):
    if cols.start < W_A_KV:
        return _epi_qk_rope(acc, cols, g_ref, cos_ref, sin_ref, out_scale=1.0)
    return acc


def _gelu_tanh(x):
    return 0.5 * x * (1.0 + jnp.tanh(0.7978845608028654 * (x + 0.044715 * (x * x * x))))


def _epi_gelu(acc, cols):
    return _gelu_tanh(acc)


def _epi_gelu_rms(acc, cols, g_ref):
    y = _gelu_tanh(acc)
    g = g_ref[:, cols]
    groups = []
    for c in range(acc.shape[1] // HEAD_DIM):
        sl = slice(c * HEAD_DIM, (c + 1) * HEAD_DIM)
        groups.append(_head_rms(y[:, sl], g[:, sl]))
    return jnp.concatenate(groups, axis=1)


def _proj_kernel(h_ref, w_ref, *rest, epilogue, chunk):
    extras, o_ref = rest[:-1], rest[-1]
    h = h_ref[...]
    for c in range(o_ref.shape[1] // chunk):
        cols = slice(c * chunk, (c + 1) * chunk)
        o_ref[:, cols] = epilogue(_dot(h, w_ref[:, cols]), cols, *extras).astype(o_ref.dtype)


def _proj(h, w_in, layer, col_off, n_cols, epilogue, extras, extra_specs, out_dtype, name, chunk=None, tn=None):
    n, d = h.shape
    tm = _tile(n, 640, 128)
    tn = _tile(n_cols, 1024, 128) if tn is None else tn
    chunk = tn if chunk is None else chunk
    assert col_off % tn == 0 and tn % chunk == 0
    off = col_off // tn
    return pl.pallas_call(
        functools.partial(_proj_kernel, epilogue=epilogue, chunk=chunk),
        grid=(n // tm, n_cols // tn),
        in_specs=[
            pl.BlockSpec((tm, d), lambda i, j: (i, 0)),
            pl.BlockSpec((None, d, tn), lambda i, j: (layer, 0, j + off)),
        ] + [spec(tm, tn) for spec in extra_specs],
        out_specs=pl.BlockSpec((tm, tn), lambda i, j: (i, j)),
        out_shape=jax.ShapeDtypeStruct((n, n_cols), out_dtype),
        compiler_params=_params("parallel", "parallel"),
        name=name,
    )(h, w_in, *extras)


def _gattn_kernel(q_ref, kt_ref, v_ref, o_ref, qs_ref, m_ref, l_ref, acc_ref, *, tq, tk, n_lat, n_ctx):
    for h in range(GQA_REP):
        qs_ref[h * tq:(h + 1) * tq, :] = q_ref[:, h * HEAD_DIM:(h + 1) * HEAD_DIM]
    m_ref[...] = jnp.full(m_ref.shape, MASK_VALUE, F32)
    l_ref[...] = jnp.zeros(l_ref.shape, F32)
    acc_ref[...] = jnp.zeros(acc_ref.shape, F32)

    def step(kt, v):
        ncol = v.shape[0] // HEAD_DIM
        s = _dot(qs_ref[...], kt)
        m_old = m_ref[...]
        m_new = jnp.maximum(m_old, jnp.max(s, axis=1, keepdims=True))
        alpha = jnp.exp2(m_old - m_new)
        p = jnp.exp2(s - jnp.concatenate([m_new] * ncol, axis=1))
        psum = p[:, :HEAD_DIM]
        for j in range(1, ncol):
            psum = psum + p[:, j * HEAD_DIM:(j + 1) * HEAD_DIM]
        l_ref[...] = alpha * l_ref[...] + psum
        acc_ref[...] = alpha * acc_ref[...] + _dot(p.astype(BF16), v)
        m_ref[...] = m_new

    def body(c, carry):
        off = pl.multiple_of(c * tk, tk)
        step(kt_ref[:, pl.ds(off, tk)], v_ref[pl.ds(off, tk), :])
        return carry

    n_chunks = (n_lat + n_ctx) // tk
    is_ctx = pl.program_id(1) * tq >= n_lat

    @pl.when(jnp.logical_not(is_ctx))
    def _():
        lax.fori_loop(0, n_chunks, body, 0, unroll=n_chunks if n_chunks <= GATTN_MAX_UNROLL else 1)

    @pl.when(is_ctx)
    def _():
        step(kt_ref[:, n_lat:n_lat + n_ctx], v_ref[n_lat:n_lat + n_ctx, :])

    o = acc_ref[...] / jnp.sum(l_ref[...], axis=1, keepdims=True)
    for h in range(GQA_REP):
        o_ref[:, h * HEAD_DIM:(h + 1) * HEAD_DIM] = o[h * tq:(h + 1) * tq].astype(o_ref.dtype)


def _global_attn(qa, ka_t, kv_a, n_lat, n_ctx):
    n = qa.shape[0]
    tq = _tile(n_ctx, 256, 8)
    assert n_lat % tq == 0 and n_ctx % tq == 0
    tk = _tile(n, GATTN_KV_CHUNK, V7X_MXU_COLS)
    gw = GQA_REP * HEAD_DIM
    kern = functools.partial(_gattn_kernel, tq=tq, tk=tk, n_lat=n_lat, n_ctx=n_ctx)
    return pl.pallas_call(
        kern,
        grid=(N_KV_A, n // tq),
        in_specs=[
            pl.BlockSpec((tq, gw), lambda g, i: (i, g)),
            pl.BlockSpec((HEAD_DIM, n), lambda g, i: (g, 0)),
            pl.BlockSpec((n, HEAD_DIM), lambda g, i: (0, N_KV_A + g)),
        ],
        out_specs=pl.BlockSpec((tq, gw), lambda g, i: (i, g)),
        out_shape=jax.ShapeDtypeStruct((n, W_A_Q), BF16),
        scratch_shapes=[
            pltpu.VMEM((GQA_REP * tq, HEAD_DIM), BF16),
            pltpu.VMEM((GQA_REP * tq, HEAD_DIM), F32),
            pltpu.VMEM((GQA_REP * tq, HEAD_DIM), F32),
            pltpu.VMEM((GQA_REP * tq, HEAD_DIM), F32),
        ],
        compiler_params=_params("parallel", "parallel"),
        name="global_attn",
    )(qa, ka_t, kv_a)


def _nbr_kernel(q_ref, *rest):
    k_refs = rest[:NBR_KV_BLOCKS]
    v_refs = rest[NBR_KV_BLOCKS:2 * NBR_KV_BLOCKS]
    kc_ref, vc_ref, bias_ref, o_ref = rest[2 * NBR_KV_BLOCKS:]
    nt = (((1,), (1,)), ((), ()))
    heads = [slice(h * HEAD_DIM, (h + 1) * HEAD_DIM) for h in range(N_HEADS_B)]
    scores = []
    for h, sl in enumerate(heads):
        k = jnp.concatenate([r[:, sl] for r in k_refs] + [kc_ref[:, sl]], axis=0)
        scores.append(lax.dot_general(q_ref[:, sl], k, nt, preferred_element_type=F32) + bias_ref[h])
    probs, sums = [], []
    for s in scores:
        p = jnp.exp2(s - jnp.max(s, axis=1, keepdims=True))
        sums.append(jnp.sum(p, axis=1, keepdims=True))
        probs.append(p.astype(BF16))
    for h, sl in enumerate(heads):
        v = jnp.concatenate([r[:, sl] for r in v_refs] + [vc_ref[:, sl]], axis=0)
        o_ref[:, sl] = (_dot(probs[h], v) / sums[h]).astype(o_ref.dtype)


def _nbr_bias_tables(rpb, n_lat, n_ctx):
    rows = n_lat // GRID_W
    nblk = n_lat // NBR_Q
    kh = min(WIN_H, rows)
    q_rows, kv_rows = NBR_Q // GRID_W, NBR_KEYS // GRID_W
    n_off_r, n_off_c = 2 * WIN_H - 1, 2 * WIN_W - 1
    blocks = (0, 1, 2, nblk - 2, nblk - 1)
    sel_r = np.zeros((len(blocks), n_off_r, q_rows, kv_rows), np.float32)
    for p, blk in enumerate(blocks):
        blk0 = min(max(blk - 2, 0), nblk - NBR_KV_BLOCKS)
        for qr in range(q_rows):
            rq = q_rows * blk + qr
            r0 = min(max(rq - kh // 2, 0), rows - kh)
            for kr in range(kv_rows):
                key_row = q_rows * blk0 + kr
                if r0 <= key_row < r0 + kh:
                    sel_r[p, key_row - rq + WIN_H - 1, qr, kr] = 1.0
    sel_c = np.zeros((n_off_c, GRID_W, GRID_W), np.float32)
    for qc in range(GRID_W):
        c0 = min(max(qc - WIN_W // 2, 0), GRID_W - WIN_W)
        for kc in range(c0, c0 + WIN_W):
            sel_c[kc - qc + WIN_W - 1, qc, kc] = 1.0
    inside = np.einsum("paqk,bcd->pqckd", sel_r, sel_c).reshape(len(blocks), NBR_Q, NBR_KEYS) > 0.5
    bias = jnp.einsum("lhab,paqk,bcd->lphqckd", rpb.astype(F32), sel_r, sel_c,
                      precision=lax.Precision.HIGHEST)
    bias = bias.reshape(rpb.shape[0], len(blocks), N_HEADS_B, NBR_Q, NBR_KEYS) * LOG2_E
    bias = jnp.where(inside[None, :, None], bias, MASK_VALUE)
    bias = jnp.concatenate([bias, jnp.full_like(bias[:, :1], MASK_VALUE)], axis=1)
    return jnp.pad(bias, ((0, 0),) * 4 + ((0, n_ctx),))


def _nbr_attn(qkv_b, bias, layer, n_lat, n_ctx):
    n = qkv_b.shape[0]
    nblk = n_lat // NBR_Q
    assert nblk >= NBR_KV_BLOCKS + 1 and n_ctx % NBR_Q == 0
    ctx_blk = n_lat // n_ctx
    assert n_lat % n_ctx == 0

    def first_kv(i):
        return jnp.clip(i - 2, 0, nblk - NBR_KV_BLOCKS)

    def pattern(i):
        return jnp.where(i >= nblk, NBR_PATTERNS - 1, i - first_kv(i))

    def kv_spec(col, b):
        return pl.BlockSpec((NBR_Q, W_B), lambda i: (first_kv(i) + b, col))

    in_specs = ([pl.BlockSpec((NBR_Q, W_B), lambda i: (i, 0))]
                + [kv_spec(1, b) for b in range(NBR_KV_BLOCKS)]
                + [kv_spec(2, b) for b in range(NBR_KV_BLOCKS)]
                + [pl.BlockSpec((n_ctx, W_B), lambda i: (ctx_blk, 1)),
                   pl.BlockSpec((n_ctx, W_B), lambda i: (ctx_blk, 2)),
                   pl.BlockSpec((None, None, N_HEADS_B, NBR_Q, NBR_KEYS + n_ctx),
                                lambda i: (layer, pattern(i), 0, 0, 0))])
    return pl.pallas_call(
        _nbr_kernel,
        grid=(n // NBR_Q,),
        in_specs=in_specs,
        out_specs=pl.BlockSpec((NBR_Q, W_B), lambda i: (i, 0)),
        out_shape=jax.ShapeDtypeStruct((n, W_B), BF16),
        compiler_params=_params("parallel"),
        name="nbr_attn",
    )(*([qkv_b] * (3 + 2 * NBR_KV_BLOCKS)), bias)


def _sgu_kernel(u_ref, vn_ref, w_ref, bt_ref, o_ref, *, n_chunks):
    for g in range(N_GROUPS_C):
        sl = slice(g * HEAD_DIM, (g + 1) * HEAD_DIM)
        w = w_ref[g]
        b = bt_ref[:, g:g + 1]
        for c in range(n_chunks):
            rs = slice(c * CHUNK, (c + 1) * CHUNK)
            sv = _dot(w, vn_ref[rs, sl]) + b
            o_ref[rs, sl] = (u_ref[rs, sl] * sv).astype(o_ref.dtype)


def _sgu(u, vn, w_s, b_t, layer):
    n = u.shape[0]
    tr = _tile(n, 1280, CHUNK)
    kern = functools.partial(_sgu_kernel, n_chunks=tr // CHUNK)
    return pl.pallas_call(
        kern,
        grid=(n // tr,),
        in_specs=[
            pl.BlockSpec((tr, W_C), lambda i: (i, 0)),
            pl.BlockSpec((tr, W_C), lambda i: (i, 0)),
            pl.BlockSpec((None, N_GROUPS_C, CHUNK, CHUNK), lambda i: (layer, 0, 0, 0)),
            pl.BlockSpec((None, CHUNK, N_GROUPS_C), lambda i: (layer, 0, 0)),
        ],
        out_specs=pl.BlockSpec((tr, W_C), lambda i: (i, 0)),
        out_shape=jax.ShapeDtypeStruct((n, W_C), BF16),
        compiler_params=_params("parallel"),
        name="sgu",
    )(u, vn, w_s, b_t)


def _out_proj_kernel(oa_ref, ob_ref, oc_ref, wa_ref, wb_ref, wc_ref, x_ref, gate_ref, o_ref, *, tm, n_lat):
    y = _dot(oa_ref[...], wa_ref[...]) + _dot(ob_ref[...], wb_ref[...]) + _dot(oc_ref[...], wc_ref[...])
    o_ref[...] = x_ref[...] + _row_select(gate_ref, tm, n_lat) * y


def _out_proj(oa, ob, oc, w_out, x, mods, layer, n_lat):
    n, d = x.shape
    tm = _tile(n, 1280, 128)
    tn = _tile(d, 512, 128)
    assert W_A_Q % W_B == 0 and W_B == W_C
    kern = functools.partial(_out_proj_kernel, tm=tm, n_lat=n_lat)
    return pl.pallas_call(
        kern,
        grid=(n // tm, d // tn),
        in_specs=[
            pl.BlockSpec((tm, W_A_Q), lambda i, j: (i, 0)),
            pl.BlockSpec((tm, W_B), lambda i, j: (i, 0)),
            pl.BlockSpec((tm, W_C), lambda i, j: (i, 0)),
            pl.BlockSpec((None, W_A_Q, tn), lambda i, j: (layer, 0, j)),
            pl.BlockSpec((None, W_B, tn), lambda i, j: (layer, W_A_Q // W_B, j)),
            pl.BlockSpec((None, W_C, tn), lambda i, j: (layer, W_A_Q // W_B + 1, j)),
            pl.BlockSpec((tm, tn), lambda i, j: (i, j)),
            pl.BlockSpec((None, None, MOD_ROWS, tn), lambda i, j: (layer, 5, 0, j)),
        ],
        out_specs=pl.BlockSpec((tm, tn), lambda i, j: (i, j)),
        out_shape=jax.ShapeDtypeStruct((n, d), F32),
        compiler_params=_params("parallel", "parallel"),
        name="out_proj",
    )(oa, ob, oc, w_out, w_out, w_out, x, mods)


def _rope_tables(n_lat, n_ctx):
    t = jnp.arange(n_lat, dtype=jnp.int32)
    row = (t // GRID_W).astype(F32)
    col = (t % GRID_W).astype(F32)
    half = HEAD_DIM // 2
    inv = ROPE_THETA ** (-jnp.arange(0, half, 2, dtype=F32) / half)
    ang_r = row[:, None] * inv[None, :]
    ang_c = col[:, None] * inv[None, :]
    ang = jnp.concatenate([ang_r, ang_r, ang_c, ang_c], axis=-1)
    lane = jnp.arange(HEAD_DIM)
    sign = jnp.where((lane % half) < HEAD_DIM // 4, -1.0, 1.0).astype(F32)
    cos = jnp.concatenate([jnp.cos(ang), jnp.ones((n_ctx, HEAD_DIM), F32)], axis=0)
    sin = jnp.concatenate([jnp.sin(ang) * sign, jnp.zeros((n_ctx, HEAD_DIM), F32)], axis=0)
    return cos, sin


def _mixer_groups(h, l, mix_w_in_b, qk_g, sgu_g, cos, sin, nbr_bias, sgu_w_b, sgu_b_t, n_lat, n_ctx):
    rope_specs = [lambda tm, tn: pl.BlockSpec((tm, HEAD_DIM), lambda i, j: (i, 0))] * 2

    def g_spec(which):
        return lambda tm, tn: pl.BlockSpec((None, None, 1, HEAD_DIM), lambda i, j: (l, which, 0, 0))

    col_ka = W_A_Q
    col_b = col_ka + 2 * W_A_KV
    col_u = col_b + 3 * W_B
    col_v = col_u + W_C
    qa = _proj(h, mix_w_in_b, l, 0, W_A_Q,
               functools.partial(_epi_qk_rope, out_scale=HEAD_DIM ** -0.5 * LOG2_E),
               [qk_g, cos, sin], [g_spec(0)] + rope_specs, BF16, "proj_qa", chunk=V7X_MXU_COLS)
    kv_a = _proj(h, mix_w_in_b, l, col_ka, 2 * W_A_KV, _epi_kv_a,
                 [qk_g, cos, sin], [g_spec(1)] + rope_specs, BF16, "proj_kv_a", chunk=V7X_MXU_COLS)
    qkv_b = _proj(h, mix_w_in_b, l, col_b, 3 * W_B, _epi_qkv_b, [], [], BF16, "proj_b", tn=W_B)
    u = _proj(h, mix_w_in_b, l, col_u, W_C, _epi_gelu, [], [], F32, "proj_u", chunk=V7X_MXU_COLS)
    vn = _proj(h, mix_w_in_b, l, col_v, W_C, _epi_gelu_rms, [sgu_g],
               [lambda tm, tn: pl.BlockSpec((None, 1, tn), lambda i, j: (l, 0, j))], BF16, "proj_v",
               chunk=V7X_MXU_COLS)
    oa = _global_attn(qa, kv_a[:, :W_A_KV].T, kv_a, n_lat, n_ctx)
    ob = _nbr_attn(qkv_b, nbr_bias, l, n_lat, n_ctx)
    oc = _sgu(u, vn, sgu_w_b, sgu_b_t, l)
    return oa, ob, oc


def kernel(x, c, ctx, c_ctx, norm_g, mod_w_down, mod_w_up, mod_b, ffn_w_in, ffn_w_out, mix_w_in, mix_w_out,
           qk_norm_g, rpb, sgu_norm_g, sgu_w, sgu_b, final_norm_g):
    batch, n_lat, d = x.shape
    n_ctx = ctx.shape[1]
    depth = norm_g.shape[0]
    assert batch == 1 and c.shape[0] == 1

    ffn_w_in_b = ffn_w_in.astype(BF16)
    ffn_w_out_b = ffn_w_out.astype(BF16)
    mix_w_in_b = mix_w_in.astype(BF16)
    mix_w_out_b = mix_w_out.astype(BF16)
    sgu_w_b = sgu_w.astype(BF16)
    sgu_b_t = jnp.swapaxes(sgu_b, 1, 2)
    norm_g4 = norm_g.reshape(depth, 3, 1, d)

    cond = jnp.zeros((MOD_ROWS, d), F32).at[0].set(c[0]).at[1].set(c_ctx)
    mods = _adaln(cond, mod_w_down, mod_w_up, mod_b)
    mods = mods.reshape(depth, MOD_ROWS, N_MOD, d).swapaxes(1, 2)

    cos, sin = _rope_tables(n_lat, n_ctx)
    nbr_bias = _nbr_bias_tables(rpb, n_lat, n_ctx)
    qk_g = qk_norm_g.reshape(depth, 2, 1, HEAD_DIM)
    sgu_g = sgu_norm_g.reshape(depth, 1, W_C)

    xs = jnp.concatenate([x[0], ctx[0]], axis=0)

    for l in range(depth):
        h = _norm_mod(xs, norm_g4, mods, l, 0, n_lat)
        xs = _ffn_down(_ffn_up(h, ffn_w_in_b, l, 0), ffn_w_out_b, xs, mods, l, 0, 2, n_lat)

        h = _norm_mod(xs, norm_g4, mods, l, 1, n_lat)
        oa, ob, oc = _mixer_groups(h, l, mix_w_in_b, qk_g, sgu_g, cos, sin, nbr_bias, sgu_w_b, sgu_b_t,
                                   n_lat, n_ctx)
        xs = _out_proj(oa, ob, oc, mix_w_out_b, xs, mods, l, n_lat)

        h = _norm_mod(xs, norm_g4, mods, l, 2, n_lat)
        xs = _ffn_down(_ffn_up(h, ffn_w_in_b, l, 1), ffn_w_out_b, xs, mods, l, 1, 8, n_lat)

    return _final_norm(xs, final_norm_g, n_lat).reshape(1, n_lat, d)
```

```python
import functools
import math

import jax
import jax.numpy as jnp
import numpy as np
from jax import lax
from jax.experimental import pallas as pl
from jax.experimental.pallas import tpu as pltpu

HEAD_DIM = 128
GRID_W = 64
N_HEADS_A = 16
N_KV_A = 4
N_HEADS_B = 8
N_GROUPS_C = 8
WIN_H = 8
WIN_W = 16
CHUNK = 128
N_MOD = 9
ROPE_THETA = 10000.0
EPS = 1e-6
W_A_Q = N_HEADS_A * HEAD_DIM
W_A_KV = N_KV_A * HEAD_DIM
W_B = N_HEADS_B * HEAD_DIM
W_C = N_GROUPS_C * HEAD_DIM
GQA_REP = N_HEADS_A // N_KV_A

NBR_Q = 2 * GRID_W
NBR_KV_BLOCKS = 5
NBR_KEYS = NBR_KV_BLOCKS * NBR_Q
NBR_PATTERNS = 6
MASK_VALUE = -1e30
LOG2_E = 1.4426950408889634

V7X_VMEM_BYTES = 64 * 1024 * 1024
VMEM_LIMIT_BYTES = V7X_VMEM_BYTES - 8 * 1024 * 1024
V7X_MXU_COLS = 256
GATTN_KV_CHUNK = 3584
GATTN_MAX_UNROLL = 8
MOD_ROWS = 8
NORM_ROWS = 16

F32 = jnp.float32
BF16 = jnp.bfloat16


def _tile(n, target, align):
    best = None
    for t in range(align, min(n, target) + 1, align):
        if n % t == 0:
            best = t
    assert best is not None, (n, target, align)
    return best


def _params(*sem):
    return pltpu.CompilerParams(dimension_semantics=sem, vmem_limit_bytes=VMEM_LIMIT_BYTES)


def _row_select(mod_ref, tm, n_lat):
    row = pl.program_id(0) * tm + lax.broadcasted_iota(jnp.int32, (tm, 1), 0)
    return jnp.where(row >= n_lat, mod_ref[1:2, :], mod_ref[0:1, :])


def _dot(a, b):
    return jnp.dot(a, b, preferred_element_type=F32)


def _adaln_kernel(cond_ref, wd_ref, wu_ref, b_ref, o_ref, mid_ref):
    @pl.when(pl.program_id(1) == 0)
    def _():
        s = cond_ref[...]
        s = s * (1.0 / (1.0 + jnp.exp(-s)))
        mid_ref[...] = jnp.dot(s, wd_ref[...], preferred_element_type=F32, precision=lax.Precision.HIGHEST)

    o_ref[...] = jnp.dot(mid_ref[...], wu_ref[...], preferred_element_type=F32,
                         precision=lax.Precision.HIGHEST) + b_ref[...]


def _adaln(cond, w_down, w_up, b_up):
    depth, d, r = w_down.shape
    n = w_up.shape[-1]
    tn = _tile(n, 4096, 128)
    return pl.pallas_call(
        _adaln_kernel,
        grid=(depth, n // tn),
        in_specs=[
            pl.BlockSpec((MOD_ROWS, d), lambda l, j: (0, 0)),
            pl.BlockSpec((None, d, r), lambda l, j: (l, 0, 0)),
            pl.BlockSpec((None, r, tn), lambda l, j: (l, 0, j)),
            pl.BlockSpec((None, 1, tn), lambda l, j: (l, 0, j)),
        ],
        out_specs=pl.BlockSpec((None, MOD_ROWS, tn), lambda l, j: (l, 0, j)),
        out_shape=jax.ShapeDtypeStruct((depth, MOD_ROWS, n), F32),
        scratch_shapes=[pltpu.VMEM((MOD_ROWS, r), F32)],
        compiler_params=_params("parallel", "arbitrary"),
        name="adaln",
    )(cond, w_down, w_up, b_up.reshape(depth, 1, n))


def _norm_mod_kernel(x_ref, g_ref, shift_ref, scale_ref, o_ref, *, tr, n_lat):
    which = (pl.program_id(0) * tr >= n_lat).astype(jnp.int32)
    gain = g_ref[...] * (1.0 + scale_ref[pl.ds(which, 1), :])
    shift = shift_ref[pl.ds(which, 1), :]

    def body(r, carry):
        rs = pl.ds(pl.multiple_of(r * NORM_ROWS, NORM_ROWS), NORM_ROWS)
        x = x_ref[rs, :]
        inv = lax.rsqrt(jnp.mean(x * x, axis=-1, keepdims=True) + EPS)
        o_ref[rs, :] = (x_ref[rs, :] * inv * gain + shift).astype(o_ref.dtype)
        return carry

    lax.fori_loop(0, tr // NORM_ROWS, body, 0, unroll=2)


def _norm_mod(x, norm_g, mods, layer, sub, n_lat):
    n, d = x.shape
    tr = _tile(math.gcd(n_lat, n - n_lat), 256, NORM_ROWS)
    kern = functools.partial(_norm_mod_kernel, tr=tr, n_lat=n_lat)
    return pl.pallas_call(
        kern,
        grid=(n // tr,),
        in_specs=[
            pl.BlockSpec((tr, d), lambda i: (i, 0)),
            pl.BlockSpec((None, None, 1, d), lambda i: (layer, sub, 0, 0)),
            pl.BlockSpec((None, None, MOD_ROWS, d), lambda i: (layer, 3 * sub, 0, 0)),
            pl.BlockSpec((None, None, MOD_ROWS, d), lambda i: (layer, 3 * sub + 1, 0, 0)),
        ],
        out_specs=pl.BlockSpec((tr, d), lambda i: (i, 0)),
        out_shape=jax.ShapeDtypeStruct((n, d), BF16),
        compiler_params=_params("parallel"),
        name="norm_mod",
    )(x, norm_g, mods, mods)


def _final_norm_kernel(x_ref, g_ref, o_ref):
    g = g_ref[...]

    def body(r, carry):
        rs = pl.ds(pl.multiple_of(r * NORM_ROWS, NORM_ROWS), NORM_ROWS)
        x = x_ref[rs, :]
        inv = lax.rsqrt(jnp.mean(x * x, axis=-1, keepdims=True) + EPS)
        o_ref[rs, :] = x_ref[rs, :] * inv * g
        return carry

    lax.fori_loop(0, x_ref.shape[0] // NORM_ROWS, body, 0, unroll=2)


def _final_norm(x, g, n_lat):
    d = x.shape[1]
    tr = _tile(n_lat, 256, NORM_ROWS)
    return pl.pallas_call(
        _final_norm_kernel,
        grid=(n_lat // tr,),
        in_specs=[pl.BlockSpec((tr, d), lambda i: (i, 0)),
                  pl.BlockSpec((1, d), lambda i: (0, 0))],
        out_specs=pl.BlockSpec((tr, d), lambda i: (i, 0)),
        out_shape=jax.ShapeDtypeStruct((n_lat, d), F32),
        compiler_params=_params("parallel"),
        name="final_norm",
    )(x, g.reshape(1, d))


def _ffn_up_kernel(h_ref, wa_ref, wb_ref, o_ref):
    h = h_ref[...]
    a = _dot(h, wa_ref[...].astype(BF16))
    b = _dot(h, wb_ref[...].astype(BF16))
    o_ref[...] = (a * (1.0 / (1.0 + jnp.exp(-a))) * b).astype(o_ref.dtype)


def _ffn_up(h, w_in, layer, half):
    n, d = h.shape
    f = w_in.shape[-1] // 2
    tm = _tile(n, 1280, 128)
    tn = _tile(f, V7X_MXU_COLS, 128)
    nb = f // tn
    return pl.pallas_call(
        _ffn_up_kernel,
        grid=(n // tm, nb),
        in_specs=[
            pl.BlockSpec((tm, d), lambda i, j: (i, 0)),
            pl.BlockSpec((None, None, d, tn), lambda i, j: (layer, half, 0, j)),
            pl.BlockSpec((None, None, d, tn), lambda i, j: (layer, half, 0, j + nb)),
        ],
        out_specs=pl.BlockSpec((tm, tn), lambda i, j: (i, j)),
        out_shape=jax.ShapeDtypeStruct((n, f), BF16),
        compiler_params=_params("parallel", "parallel"),
        name="ffn_up",
    )(h, w_in, w_in)


def _ffn_down_kernel(g_ref, w_ref, x_ref, gate_ref, o_ref, *, tm, n_lat):
    y = _dot(g_ref[...], w_ref[...])
    o_ref[...] = x_ref[...] + 0.5 * _row_select(gate_ref, tm, n_lat) * y


def _ffn_down(g, w_out, x, mods, layer, half, gate_idx, n_lat):
    n, f = g.shape
    d = x.shape[1]
    tm = _tile(n, 640, 128)
    tn = _tile(d, 512, 128)
    kern = functools.partial(_ffn_down_kernel, tm=tm, n_lat=n_lat)
    return pl.pallas_call(
        kern,
        grid=(n // tm, d // tn),
        in_specs=[
            pl.BlockSpec((tm, f), lambda i, j: (i, 0)),
            pl.BlockSpec((None, None, f, tn), lambda i, j: (layer, half, 0, j)),
            pl.BlockSpec((tm, tn), lambda i, j: (i, j)),
            pl.BlockSpec((None, None, MOD_ROWS, tn), lambda i, j: (layer, gate_idx, 0, j)),
        ],
        out_specs=pl.BlockSpec((tm, tn), lambda i, j: (i, j)),
        out_shape=jax.ShapeDtypeStruct((n, d), F32),
        compiler_params=_params("parallel", "parallel"),
        name="ffn_down",
    )(g, w_out, x, mods)


def _head_rms(x, g):
    return x * lax.rsqrt(jnp.mean(x * x, axis=-1, keepdims=True) + EPS) * g


def _rope(x, cos, sin_signed):
    lane = lax.broadcasted_iota(jnp.int32, x.shape, 1)
    low = jnp.bitwise_and(lane, HEAD_DIM // 2 - 1) < (HEAD_DIM // 4)
    rot = jnp.where(low, pltpu.roll(x, HEAD_DIM - HEAD_DIM // 4, 1), pltpu.roll(x, HEAD_DIM // 4, 1))
    return x * cos + rot * sin_signed


def _epi_qk_rope(acc, cols, g_ref, cos_ref, sin_ref, *, out_scale):
    g = g_ref[...]
    cos = cos_ref[...]
    sin = sin_ref[...]
    heads = []
    for h in range(acc.shape[1] // HEAD_DIM):
        x = _head_rms(acc[:, h * HEAD_DIM:(h + 1) * HEAD_DIM], g)
        heads.append(_rope(x, cos, sin) * out_scale)
    return jnp.concatenate(heads, axis=1)


def _epi_plain(acc, cols):
    return acc


def _epi_qkv_b(acc, cols):
    return acc * jnp.where(pl.program_id(1) == 0, HEAD_DIM ** -0.5 * LOG2_E, 1.0)


def _epi_kv_a(acc, cols, g_ref, cos_ref, sin_ref):
    if cols.start < W_A_KV:
        return _epi_qk_rope(acc, cols, g_ref, cos_ref, sin_ref, out_scale=1.0)
    return acc


def _gelu_tanh(x):
    return 0.5 * x * (1.0 + jnp.tanh(0.7978845608028654 * (x + 0.044715 * (x * x * x))))


def _epi_gelu(acc, cols):
    return _gelu_tanh(acc)


def _epi_gelu_rms(acc, cols, g_ref):
    y = _gelu_tanh(acc)
    g = g_ref[:, cols]
    groups = []
    for c in range(acc.shape[1] // HEAD_DIM):
        sl = slice(c * HEAD_DIM, (c + 1) * HEAD_DIM)
        groups.append(_head_rms(y[:, sl], g[:, sl]))
    return jnp.concatenate(groups, axis=1)


def _proj_kernel(h_ref, w_ref, *rest, epilogue, chunk):
    extras, o_ref = rest[:-1], rest[-1]
    h = h_ref[...]
    for c in range(o_ref.shape[1] // chunk):
        cols = slice(c * chunk, (c + 1) * chunk)
        o_ref[:, cols] = epilogue(_dot(h, w_ref[:, cols]), cols, *extras).astype(o_ref.dtype)


def _proj(h, w_in, layer, col_off, n_cols, epilogue, extras, extra_specs, out_dtype, name, chunk=None, tn=None):
    n, d = h.shape
    tm = _tile(n, 640, 128)
    tn = _tile(n_cols, 1024, 128) if tn is None else tn
    chunk = tn if chunk is None else chunk
    assert col_off % tn == 0 and tn % chunk == 0
    off = col_off // tn
    return pl.pallas_call(
        functools.partial(_proj_kernel, epilogue=epilogue, chunk=chunk),
        grid=(n // tm, n_cols // tn),
        in_specs=[
            pl.BlockSpec((tm, d), lambda i, j: (i, 0)),
            pl.BlockSpec((None, d, tn), lambda i, j: (layer, 0, j + off)),
        ] + [spec(tm, tn) for spec in extra_specs],
        out_specs=pl.BlockSpec((tm, tn), lambda i, j: (i, j)),
        out_shape=jax.ShapeDtypeStruct((n, n_cols), out_dtype),
        compiler_params=_params("parallel", "parallel"),
        name=name,
    )(h, w_in, *extras)


def _gattn_kernel(q_ref, kt_ref, v_ref, o_ref, qs_ref, m_ref, l_ref, acc_ref, *, tq, tk, n_lat, n_ctx):
    for h in range(GQA_REP):
        qs_ref[h * tq:(h + 1) * tq, :] = q_ref[:, h * HEAD_DIM:(h + 1) * HEAD_DIM]
    m_ref[...] = jnp.full(m_ref.shape, MASK_VALUE, F32)
    l_ref[...] = jnp.zeros(l_ref.shape, F32)
    acc_ref[...] = jnp.zeros(acc_ref.shape, F32)

    def step(kt, v):
        ncol = v.shape[0] // HEAD_DIM
        s = _dot(qs_ref[...], kt)
        m_old = m_ref[...]
        m_new = jnp.maximum(m_old, jnp.max(s, axis=1, keepdims=True))
        alpha = jnp.exp2(m_old - m_new)
        p = jnp.exp2(s - jnp.concatenate([m_new] * ncol, axis=1))
        psum = p[:, :HEAD_DIM]
        for j in range(1, ncol):
            psum = psum + p[:, j * HEAD_DIM:(j + 1) * HEAD_DIM]
        l_ref[...] = alpha * l_ref[...] + psum
        acc_ref[...] = alpha * acc_ref[...] + _dot(p.astype(BF16), v)
        m_ref[...] = m_new

    def body(c, carry):
        off = pl.multiple_of(c * tk, tk)
        step(kt_ref[:, pl.ds(off, tk)], v_ref[pl.ds(off, tk), :])
        return carry

    n_chunks = (n_lat + n_ctx) // tk
    is_ctx = pl.program_id(1) * tq >= n_lat

    @pl.when(jnp.logical_not(is_ctx))
    def _():
        lax.fori_loop(0, n_chunks, body, 0, unroll=n_chunks if n_chunks <= GATTN_MAX_UNROLL else 1)

    @pl.when(is_ctx)
    def _():
        step(kt_ref[:, n_lat:n_lat + n_ctx], v_ref[n_lat:n_lat + n_ctx, :])

    o = acc_ref[...] / jnp.sum(l_ref[...], axis=1, keepdims=True)
    for h in range(GQA_REP):
        o_ref[:, h * HEAD_DIM:(h + 1) * HEAD_DIM] = o[h * tq:(h + 1) * tq].astype(o_ref.dtype)


def _global_attn(qa, ka_t, kv_a, n_lat, n_ctx):
    n = qa.shape[0]
    tq = _tile(n_ctx, 256, 8)
    assert n_lat % tq == 0 and n_ctx % tq == 0
    tk = _tile(n, GATTN_KV_CHUNK, V7X_MXU_COLS)
    gw = GQA_REP * HEAD_DIM
    kern = functools.partial(_gattn_kernel, tq=tq, tk=tk, n_lat=n_lat, n_ctx=n_ctx)
    return pl.pallas_call(
        kern,
        grid=(N_KV_A, n // tq),
        in_specs=[
            pl.BlockSpec((tq, gw), lambda g, i: (i, g)),
            pl.BlockSpec((HEAD_DIM, n), lambda g, i: (g, 0)),
            pl.BlockSpec((n, HEAD_DIM), lambda g, i: (0, N_KV_A + g)),
        ],
        out_specs=pl.BlockSpec((tq, gw), lambda g, i: (i, g)),
        out_shape=jax.ShapeDtypeStruct((n, W_A_Q), BF16),
        scratch_shapes=[
            pltpu.VMEM((GQA_REP * tq, HEAD_DIM), BF16),
            pltpu.VMEM((GQA_REP * tq, HEAD_DIM), F32),
            pltpu.VMEM((GQA_REP * tq, HEAD_DIM), F32),
            pltpu.VMEM((GQA_REP * tq, HEAD_DIM), F32),
        ],
        compiler_params=_params("parallel", "parallel"),
        name="global_attn",
    )(qa, ka_t, kv_a)


def _nbr_kernel(q_ref, *rest):
    k_refs = rest[:NBR_KV_BLOCKS]
    v_refs = rest[NBR_KV_BLOCKS:2 * NBR_KV_BLOCKS]
    kc_ref, vc_ref, bias_ref, o_ref = rest[2 * NBR_KV_BLOCKS:]
    nt = (((1,), (1,)), ((), ()))
    heads = [slice(h * HEAD_DIM, (h + 1) * HEAD_DIM) for h in range(N_HEADS_B)]
    scores = []
    for h, sl in enumerate(heads):
        k = jnp.concatenate([r[:, sl] for r in k_refs] + [kc_ref[:, sl]], axis=0)
        scores.append(lax.dot_general(q_ref[:, sl], k, nt, preferred_element_type=F32) + bias_ref[h])
    probs, sums = [], []
    for s in scores:
        p = jnp.exp2(s - jnp.max(s, axis=1, keepdims=True))
        sums.append(jnp.sum(p, axis=1, keepdims=True))
        probs.append(p.astype(BF16))
    for h, sl in enumerate(heads):
        v = jnp.concatenate([r[:, sl] for r in v_refs] + [vc_ref[:, sl]], axis=0)
        o_ref[:, sl] = (_dot(probs[h], v) / sums[h]).astype(o_ref.dtype)


def _nbr_bias_tables(rpb, n_lat, n_ctx):
    rows = n_lat // GRID_W
    nblk = n_lat // NBR_Q
    kh = min(WIN_H, rows)
    q_rows, kv_rows = NBR_Q // GRID_W, NBR_KEYS // GRID_W
    n_off_r, n_off_c = 2 * WIN_H - 1, 2 * WIN_W - 1
    blocks = (0, 1, 2, nblk - 2, nblk - 1)
    sel_r = np.zeros((len(blocks), n_off_r, q_rows, kv_rows), np.float32)
    for p, blk in enumerate(blocks):
        blk0 = min(max(blk - 2, 0), nblk - NBR_KV_BLOCKS)
        for qr in range(q_rows):
            rq = q_rows * blk + qr
            r0 = min(max(rq - kh // 2, 0), rows - kh)
            for kr in range(kv_rows):
                key_row = q_rows * blk0 + kr
                if r0 <= key_row < r0 + kh:
                    sel_r[p, key_row - rq + WIN_H - 1, qr, kr] = 1.0
    sel_c = np.zeros((n_off_c, GRID_W, GRID_W), np.float32)
    for qc in range(GRID_W):
        c0 = min(max(qc - WIN_W // 2, 0), GRID_W - WIN_W)
        for kc in range(c0, c0 + WIN_W):
            sel_c[kc - qc + WIN_W - 1, qc, kc] = 1.0
    inside = np.einsum("paqk,bcd->pqckd", sel_r, sel_c).reshape(len(blocks), NBR_Q, NBR_KEYS) > 0.5
    bias = jnp.einsum("lhab,paqk,bcd->lphqckd", rpb.astype(F32), sel_r, sel_c,
                      precision=lax.Precision.HIGHEST)
    bias = bias.reshape(rpb.shape[0], len(blocks), N_HEADS_B, NBR_Q, NBR_KEYS) * LOG2_E
    bias = jnp.where(inside[None, :, None], bias, MASK_VALUE)
    bias = jnp.concatenate([bias, jnp.full_like(bias[:, :1], MASK_VALUE)], axis=1)
    return jnp.pad(bias, ((0, 0),) * 4 + ((0, n_ctx),))


def _nbr_attn(qkv_b, bias, layer, n_lat, n_ctx):
    n = qkv_b.shape[0]
    nblk = n_lat // NBR_Q
    assert nblk >= NBR_KV_BLOCKS + 1 and n_ctx % NBR_Q == 0
    ctx_blk = n_lat // n_ctx
    assert n_lat % n_ctx == 0

    def first_kv(i):
        return jnp.clip(i - 2, 0, nblk - NBR_KV_BLOCKS)

    def pattern(i):
        return jnp.where(i >= nblk, NBR_PATTERNS - 1, i - first_kv(i))

    def kv_spec(col, b):
        return pl.BlockSpec((NBR_Q, W_B), lambda i: (first_kv(i) + b, col))

    in_specs = ([pl.BlockSpec((NBR_Q, W_B), lambda i: (i, 0))]
                + [kv_spec(1, b) for b in range(NBR_KV_BLOCKS)]
                + [kv_spec(2, b) for b in range(NBR_KV_BLOCKS)]
                + [pl.BlockSpec((n_ctx, W_B), lambda i: (ctx_blk, 1)),
                   pl.BlockSpec((n_ctx, W_B), lambda i: (ctx_blk, 2)),
                   pl.BlockSpec((None, None, N_HEADS_B, NBR_Q, NBR_KEYS + n_ctx),
                                lambda i: (layer, pattern(i), 0, 0, 0))])
    return pl.pallas_call(
        _nbr_kernel,
        grid=(n // NBR_Q,),
        in_specs=in_specs,
        out_specs=pl.BlockSpec((NBR_Q, W_B), lambda i: (i, 0)),
        out_shape=jax.ShapeDtypeStruct((n, W_B), BF16),
        compiler_params=_params("parallel"),
        name="nbr_attn",
    )(*([qkv_b] * (3 + 2 * NBR_KV_BLOCKS)), bias)


def _sgu_kernel(u_ref, vn_ref, w_ref, bt_ref, o_ref, *, n_chunks):
    for g in range(N_GROUPS_C):
        sl = slice(g * HEAD_DIM, (g + 1) * HEAD_DIM)
        w = w_ref[g]
        b = bt_ref[:, g:g + 1]
        for c in range(n_chunks):
            rs = slice(c * CHUNK, (c + 1) * CHUNK)
            sv = _dot(w, vn_ref[rs, sl]) + b
            o_ref[rs, sl] = (u_ref[rs, sl] * sv).astype(o_ref.dtype)


def _sgu(u, vn, w_s, b_t, layer):
    n = u.shape[0]
    tr = _tile(n, 1280, CHUNK)
    kern = functools.partial(_sgu_kernel, n_chunks=tr // CHUNK)
    return pl.pallas_call(
        kern,
        grid=(n // tr,),
        in_specs=[
            pl.BlockSpec((tr, W_C), lambda i: (i, 0)),
            pl.BlockSpec((tr, W_C), lambda i: (i, 0)),
            pl.BlockSpec((None, N_GROUPS_C, CHUNK, CHUNK), lambda i: (layer, 0, 0, 0)),
            pl.BlockSpec((None, CHUNK, N_GROUPS_C), lambda i: (layer, 0, 0)),
        ],
        out_specs=pl.BlockSpec((tr, W_C), lambda i: (i, 0)),
        out_shape=jax.ShapeDtypeStruct((n, W_C), BF16),
        compiler_params=_params("parallel"),
        name="sgu",
    )(u, vn, w_s, b_t)


def _out_proj_kernel(oa_ref, ob_ref, oc_ref, wa_ref, wb_ref, wc_ref, x_ref, gate_ref, o_ref, *, tm, n_lat):
    y = _dot(oa_ref[...], wa_ref[...]) + _dot(ob_ref[...], wb_ref[...]) + _dot(oc_ref[...], wc_ref[...])
    o_ref[...] = x_ref[...] + _row_select(gate_ref, tm, n_lat) * y


def _out_proj(oa, ob, oc, w_out, x, mods, layer, n_lat):
    n, d = x.shape
    tm = _tile(n, 1280, 128)
    tn = _tile(d, 512, 128)
    assert W_A_Q % W_B == 0 and W_B == W_C
    kern = functools.partial(_out_proj_kernel, tm=tm, n_lat=n_lat)
    return pl.pallas_call(
        kern,
        grid=(n // tm, d // tn),
        in_specs=[
            pl.BlockSpec((tm, W_A_Q), lambda i, j: (i, 0)),
            pl.BlockSpec((tm, W_B), lambda i, j: (i, 0)),
            pl.BlockSpec((tm, W_C), lambda i, j: (i, 0)),
            pl.BlockSpec((None, W_A_Q, tn), lambda i, j: (layer, 0, j)),
            pl.BlockSpec((None, W_B, tn), lambda i, j: (layer, W_A_Q // W_B, j)),
            pl.BlockSpec((None, W_C, tn), lambda i, j: (layer, W_A_Q // W_B + 1, j)),
            pl.BlockSpec((tm, tn), lambda i, j: (i, j)),
            pl.BlockSpec((None, None, MOD_ROWS, tn), lambda i, j: (layer, 5, 0, j)),
        ],
        out_specs=pl.BlockSpec((tm, tn), lambda i, j: (i, j)),
        out_shape=jax.ShapeDtypeStruct((n, d), F32),
        compiler_params=_params("parallel", "parallel"),
        name="out_proj",
    )(oa, ob, oc, w_out, w_out, w_out, x, mods)


def _rope_tables(n_lat, n_ctx):
    t = jnp.arange(n_lat, dtype=jnp.int32)
    row = (t // GRID_W).astype(F32)
    col = (t % GRID_W).astype(F32)
    half = HEAD_DIM // 2
    inv = ROPE_THETA ** (-jnp.arange(0, half, 2, dtype=F32) / half)
    ang_r = row[:, None] * inv[None, :]
    ang_c = col[:, None] * inv[None, :]
    ang = jnp.concatenate([ang_r, ang_r, ang_c, ang_c], axis=-1)
    lane = jnp.arange(HEAD_DIM)
    sign = jnp.where((lane % half) < HEAD_DIM // 4, -1.0, 1.0).astype(F32)
    cos = jnp.concatenate([jnp.cos(ang), jnp.ones((n_ctx, HEAD_DIM), F32)], axis=0)
    sin = jnp.concatenate([jnp.sin(ang) * sign, jnp.zeros((n_ctx, HEAD_DIM), F32)], axis=0)
    return cos, sin


def _mixer_groups(h, l, mix_w_in_b, qk_g, sgu_g, cos, sin, nbr_bias, sgu_w_b, sgu_b_t, n_lat, n_ctx):
    rope_specs = [lambda tm, tn: pl.BlockSpec((tm, HEAD_DIM), lambda i, j: (i, 0))] * 2

    def g_spec(which):
        return lambda tm, tn: pl.BlockSpec((None, None, 1, HEAD_DIM), lambda i, j: (l, which, 0, 0))

    col_ka = W_A_Q
    col_b = col_ka + 2 * W_A_KV
    col_u = col_b + 3 * W_B
    col_v = col_u + W_C
    qa = _proj(h, mix_w_in_b, l, 0, W_A_Q,
               functools.partial(_epi_qk_rope, out_scale=HEAD_DIM ** -0.5 * LOG2_E),
               [qk_g, cos, sin], [g_spec(0)] + rope_specs, BF16, "proj_qa", chunk=V7X_MXU_COLS)
    kv_a = _proj(h, mix_w_in_b, l, col_ka, 2 * W_A_KV, _epi_kv_a,
                 [qk_g, cos, sin], [g_spec(1)] + rope_specs, BF16, "proj_kv_a", chunk=V7X_MXU_COLS)
    qkv_b = _proj(h, mix_w_in_b, l, col_b, 3 * W_B, _epi_qkv_b, [], [], BF16, "proj_b", tn=W_B)
    u = _proj(h, mix_w_in_b, l, col_u, W_C, _epi_gelu, [], [], F32, "proj_u", chunk=V7X_MXU_COLS)
    vn = _proj(h, mix_w_in_b, l, col_v, W_C, _epi_gelu_rms, [sgu_g],
               [lambda tm, tn: pl.BlockSpec((None, 1, tn), lambda i, j: (l, 0, j))], BF16, "proj_v",
               chunk=V7X_MXU_COLS)
    oa = _global_attn(qa, kv_a[:, :W_A_KV].T, kv_a, n_lat, n_ctx)
    ob = _nbr_attn(qkv_b, nbr_bias, l, n_lat, n_ctx)
    oc = _sgu(u, vn, sgu_w_b, sgu_b_t, l)
    return oa, ob, oc


def kernel(x, c, ctx, c_ctx, norm_g, mod_w_down, mod_w_up, mod_b, ffn_w_in, ffn_w_out, mix_w_in, mix_w_out,
           qk_norm_g, rpb, sgu_norm_g, sgu_w, sgu_b, final_norm_g):
    batch, n_lat, d = x.shape
    n_ctx = ctx.shape[1]
    depth = norm_g.shape[0]
    assert batch == 1 and c.shape[0] == 1

    ffn_w_out_b = ffn_w_out.astype(BF16)
    mix_w_in_b = mix_w_in.astype(BF16)
    mix_w_out_b = mix_w_out.astype(BF16)
    sgu_w_b = sgu_w.astype(BF16)
    sgu_b_t = jnp.swapaxes(sgu_b, 1, 2)
    norm_g4 = norm_g.reshape(depth, 3, 1, d)

    cond = jnp.zeros((MOD_ROWS, d), F32).at[0].set(c[0]).at[1].set(c_ctx)
    mods = _adaln(cond, mod_w_down, mod_w_up, mod_b)
    mods = mods.reshape(depth, MOD_ROWS, N_MOD, d).swapaxes(1, 2)

    cos, sin = _rope_tables(n_lat, n_ctx)
    nbr_bias = _nbr_bias_tables(rpb, n_lat, n_ctx)
    qk_g = qk_norm_g.reshape(depth, 2, 1, HEAD_DIM)
    sgu_g = sgu_norm_g.reshape(depth, 1, W_C)

    xs = jnp.concatenate([x[0], ctx[0]], axis=0)

    for l in range(depth):
        h = _norm_mod(xs, norm_g4, mods, l, 0, n_lat)
        xs = _ffn_down(_ffn_up(h, ffn_w_in, l, 0), ffn_w_out_b, xs, mods, l, 0, 2, n_lat)

        h = _norm_mod(xs, norm_g4, mods, l, 1, n_lat)
        oa, ob, oc = _mixer_groups(h, l, mix_w_in_b, qk_g, sgu_g, cos, sin, nbr_bias, sgu_w_b, sgu_b_t,
                                   n_lat, n_ctx)
        xs = _out_proj(oa, ob, oc, mix_w_out_b, xs, mods, l, n_lat)

        h = _norm_mod(xs, norm_g4, mods, l, 2, n_lat)
        xs = _ffn_down(_ffn_up(h, ffn_w_in, l, 1), ffn_w_out_b, xs, mods, l, 1, 8, n_lat)

    return _final_norm(xs, final_norm_g, n_lat).reshape(1, n_lat, d)
```

```python
import functools
import math

import jax
import jax.numpy as jnp
import numpy as np
from jax import lax
from jax.experimental import pallas as pl
from jax.experimental.pallas import tpu as pltpu

HEAD_DIM = 128
GRID_W = 64
N_HEADS_A = 16
N_KV_A = 4
N_HEADS_B = 8
N_GROUPS_C = 8
WIN_H = 8
WIN_W = 16
CHUNK = 128
N_MOD = 9
ROPE_THETA = 10000.0
EPS = 1e-6
W_A_Q = N_HEADS_A * HEAD_DIM
W_A_KV = N_KV_A * HEAD_DIM
W_B = N_HEADS_B * HEAD_DIM
W_C = N_GROUPS_C * HEAD_DIM
GQA_REP = N_HEADS_A // N_KV_A

NBR_Q = 2 * GRID_W
NBR_KV_BLOCKS = 5
NBR_KEYS = NBR_KV_BLOCKS * NBR_Q
NBR_PATTERNS = 6
MASK_VALUE = -1e30
LOG2_E = 1.4426950408889634

V7X_VMEM_BYTES = 64 * 1024 * 1024
VMEM_LIMIT_BYTES = V7X_VMEM_BYTES - 8 * 1024 * 1024
V7X_MXU_COLS = 256
GATTN_KV_CHUNK = 3584
GATTN_MAX_UNROLL = 8
MOD_ROWS = 8
BF16_SUBLANES = 16
NORM_ROWS = BF16_SUBLANES

F32 = jnp.float32
BF16 = jnp.bfloat16


def _tile(n, target, align):
    best = None
    for t in range(align, min(n, target) + 1, align):
        if n % t == 0:
            best = t
    assert best is not None, (n, target, align)
    return best


def _params(*sem):
    return pltpu.CompilerParams(dimension_semantics=sem, vmem_limit_bytes=VMEM_LIMIT_BYTES)


def _row_select(mod_ref, tm, n_lat):
    row = pl.program_id(0) * tm + lax.broadcasted_iota(jnp.int32, (tm, 1), 0)
    return jnp.where(row >= n_lat, mod_ref[1:2, :], mod_ref[0:1, :])


def _dot(a, b):
    return jnp.dot(a, b, preferred_element_type=F32)


def _adaln_kernel(cond_ref, wd_ref, wu_ref, b_ref, o_ref, mid_ref):
    @pl.when(pl.program_id(1) == 0)
    def _():
        s = cond_ref[...]
        s = s * (1.0 / (1.0 + jnp.exp(-s)))
        mid_ref[...] = jnp.dot(s, wd_ref[...], preferred_element_type=F32, precision=lax.Precision.HIGHEST)

    o_ref[...] = jnp.dot(mid_ref[...], wu_ref[...], preferred_element_type=F32,
                         precision=lax.Precision.HIGHEST) + b_ref[...]


def _adaln(cond, w_down, w_up, b_up):
    depth, d, r = w_down.shape
    n = w_up.shape[-1]
    tn = _tile(n, 4096, 128)
    return pl.pallas_call(
        _adaln_kernel,
        grid=(depth, n // tn),
        in_specs=[
            pl.BlockSpec((MOD_ROWS, d), lambda l, j: (0, 0)),
            pl.BlockSpec((None, d, r), lambda l, j: (l, 0, 0)),
            pl.BlockSpec((None, r, tn), lambda l, j: (l, 0, j)),
            pl.BlockSpec((None, 1, tn), lambda l, j: (l, 0, j)),
        ],
        out_specs=pl.BlockSpec((None, MOD_ROWS, tn), lambda l, j: (l, 0, j)),
        out_shape=jax.ShapeDtypeStruct((depth, MOD_ROWS, n), F32),
        scratch_shapes=[pltpu.VMEM((MOD_ROWS, r), F32)],
        compiler_params=_params("parallel", "arbitrary"),
        name="adaln",
    )(cond, w_down, w_up, b_up.reshape(depth, 1, n))


def _norm_mod_kernel(x_ref, g_ref, shift_ref, scale_ref, o_ref, *, tr, n_lat):
    which = (pl.program_id(0) * tr >= n_lat).astype(jnp.int32)
    gain = g_ref[...] * (1.0 + scale_ref[pl.ds(which, 1), :])
    shift = shift_ref[pl.ds(which, 1), :]

    def body(r, carry):
        rs = pl.ds(pl.multiple_of(r * NORM_ROWS, NORM_ROWS), NORM_ROWS)
        x = x_ref[rs, :]
        inv = lax.rsqrt(jnp.mean(x * x, axis=-1, keepdims=True) + EPS)
        o_ref[rs, :] = (x_ref[rs, :] * inv * gain + shift).astype(o_ref.dtype)
        return carry

    lax.fori_loop(0, tr // NORM_ROWS, body, 0, unroll=2)


def _norm_mod(x, norm_g, mods, layer, sub, n_lat):
    n, d = x.shape
    tr = _tile(math.gcd(n_lat, n - n_lat), 256, NORM_ROWS)
    kern = functools.partial(_norm_mod_kernel, tr=tr, n_lat=n_lat)
    return pl.pallas_call(
        kern,
        grid=(n // tr,),
        in_specs=[
            pl.BlockSpec((tr, d), lambda i: (i, 0)),
            pl.BlockSpec((None, None, 1, d), lambda i: (layer, sub, 0, 0)),
            pl.BlockSpec((None, None, MOD_ROWS, d), lambda i: (layer, 3 * sub, 0, 0)),
            pl.BlockSpec((None, None, MOD_ROWS, d), lambda i: (layer, 3 * sub + 1, 0, 0)),
        ],
        out_specs=pl.BlockSpec((tr, d), lambda i: (i, 0)),
        out_shape=jax.ShapeDtypeStruct((n, d), BF16),
        compiler_params=_params("parallel"),
        name="norm_mod",
    )(x, norm_g, mods, mods)


def _final_norm_kernel(x_ref, g_ref, o_ref):
    g = g_ref[...]

    def body(r, carry):
        rs = pl.ds(pl.multiple_of(r * NORM_ROWS, NORM_ROWS), NORM_ROWS)
        x = x_ref[rs, :]
        inv = lax.rsqrt(jnp.mean(x * x, axis=-1, keepdims=True) + EPS)
        o_ref[rs, :] = x_ref[rs, :] * inv * g
        return carry

    lax.fori_loop(0, x_ref.shape[0] // NORM_ROWS, body, 0, unroll=2)


def _final_norm(x, g, n_lat):
    d = x.shape[1]
    tr = _tile(n_lat, 256, NORM_ROWS)
    return pl.pallas_call(
        _final_norm_kernel,
        grid=(n_lat // tr,),
        in_specs=[pl.BlockSpec((tr, d), lambda i: (i, 0)),
                  pl.BlockSpec((1, d), lambda i: (0, 0))],
        out_specs=pl.BlockSpec((tr, d), lambda i: (i, 0)),
        out_shape=jax.ShapeDtypeStruct((n_lat, d), F32),
        compiler_params=_params("parallel"),
        name="final_norm",
    )(x, g.reshape(1, d))


def _ffn_up_kernel(h_ref, wa_ref, wb_ref, o_ref):
    h = h_ref[...]
    a = _dot(h, wa_ref[...].astype(BF16))
    b = _dot(h, wb_ref[...].astype(BF16))
    o_ref[...] = (a * (1.0 / (1.0 + jnp.exp(-a))) * b).astype(o_ref.dtype)


def _ffn_up(h, w_in, layer, half):
    n, d = h.shape
    f = w_in.shape[-1] // 2
    tm = _tile(n, 1280, 128)
    tn = _tile(f, V7X_MXU_COLS, 128)
    nb = f // tn
    return pl.pallas_call(
        _ffn_up_kernel,
        grid=(n // tm, nb),
        in_specs=[
            pl.BlockSpec((tm, d), lambda i, j: (i, 0)),
            pl.BlockSpec((None, None, d, tn), lambda i, j: (layer, half, 0, j)),
            pl.BlockSpec((None, None, d, tn), lambda i, j: (layer, half, 0, j + nb)),
        ],
        out_specs=pl.BlockSpec((tm, tn), lambda i, j: (i, j)),
        out_shape=jax.ShapeDtypeStruct((n, f), BF16),
        compiler_params=_params("parallel", "parallel"),
        name="ffn_up",
    )(h, w_in, w_in)


def _ffn_down_kernel(g_ref, w_ref, x_ref, gate_ref, o_ref, *, tm, n_lat):
    y = _dot(g_ref[...], w_ref[...])
    o_ref[...] = x_ref[...] + 0.5 * _row_select(gate_ref, tm, n_lat) * y


def _ffn_down(g, w_out, x, mods, layer, half, gate_idx, n_lat):
    n, f = g.shape
    d = x.shape[1]
    tm = _tile(n, 1040, BF16_SUBLANES)
    tn = _tile(d, 512, 128)
    kern = functools.partial(_ffn_down_kernel, tm=tm, n_lat=n_lat)
    return pl.pallas_call(
        kern,
        grid=(n // tm, d // tn),
        in_specs=[
            pl.BlockSpec((tm, f), lambda i, j: (i, 0)),
            pl.BlockSpec((None, None, f, tn), lambda i, j: (layer, half, 0, j)),
            pl.BlockSpec((tm, tn), lambda i, j: (i, j)),
            pl.BlockSpec((None, None, MOD_ROWS, tn), lambda i, j: (layer, gate_idx, 0, j)),
        ],
        out_specs=pl.BlockSpec((tm, tn), lambda i, j: (i, j)),
        out_shape=jax.ShapeDtypeStruct((n, d), F32),
        compiler_params=_params("parallel", "parallel"),
        name="ffn_down",
    )(g, w_out, x, mods)


def _head_rms(x, g):
    return x * lax.rsqrt(jnp.mean(x * x, axis=-1, keepdims=True) + EPS) * g


def _rope(x, cos, sin_signed):
    lane = lax.broadcasted_iota(jnp.int32, x.shape, 1)
    low = jnp.bitwise_and(lane, HEAD_DIM // 2 - 1) < (HEAD_DIM // 4)
    rot = jnp.where(low, pltpu.roll(x, HEAD_DIM - HEAD_DIM // 4, 1), pltpu.roll(x, HEAD_DIM // 4, 1))
    return x * cos + rot * sin_signed


def _epi_qk_rope(acc, cols, g_ref, cos_ref, sin_ref, *, out_scale):
    g = g_ref[...]
    cos = cos_ref[...]
    sin = sin_ref[...]
    heads = []
    for h in range(acc.shape[1] // HEAD_DIM):
        x = _head_rms(acc[:, h * HEAD_DIM:(h + 1) * HEAD_DIM], g)
        heads.append(_rope(x, cos, sin) * out_scale)
    return jnp.concatenate(heads, axis=1)


def _epi_plain(acc, cols):
    return acc


def _epi_qkv_b(acc, cols):
    return acc * jnp.where(pl.program_id(1) == 0, HEAD_DIM ** -0.5 * LOG2_E, 1.0)


def _epi_kv_a(acc, cols, g_ref, cos_ref, sin_ref):
    if cols.start < W_A_KV:
        return _epi_qk_rope(acc, cols, g_ref, cos_ref, sin_ref, out_scale=1.0)
    return acc


def _gelu_tanh(x):
    return 0.5 * x * (1.0 + jnp.tanh(0.7978845608028654 * (x + 0.044715 * (x * x * x))))


def _epi_gelu(acc, cols):
    return _gelu_tanh(acc)


def _epi_gelu_rms(acc, cols, g_ref):
    y = _gelu_tanh(acc)
    g = g_ref[:, cols]
    groups = []
    for c in range(acc.shape[1] // HEAD_DIM):
        sl = slice(c * HEAD_DIM, (c + 1) * HEAD_DIM)
        groups.append(_head_rms(y[:, sl], g[:, sl]))
    return jnp.concatenate(groups, axis=1)


def _proj_kernel(h_ref, w_ref, *rest, epilogue, chunk):
    extras, o_ref = rest[:-1], rest[-1]
    h = h_ref[...]
    for c in range(o_ref.shape[1] // chunk):
        cols = slice(c * chunk, (c + 1) * chunk)
        o_ref[:, cols] = epilogue(_dot(h, w_ref[:, cols]), cols, *extras).astype(o_ref.dtype)


def _proj(h, w_in, layer, col_off, n_cols, epilogue, extras, extra_specs, out_dtype, name, chunk=None, tn=None):
    n, d = h.shape
    tm = _tile(n, 1040, BF16_SUBLANES)
    tn = _tile(n_cols, 1024, 128) if tn is None else tn
    chunk = tn if chunk is None else chunk
    assert col_off % tn == 0 and tn % chunk == 0
    off = col_off // tn
    return pl.pallas_call(
        functools.partial(_proj_kernel, epilogue=epilogue, chunk=chunk),
        grid=(n // tm, n_cols // tn),
        in_specs=[
            pl.BlockSpec((tm, d), lambda i, j: (i, 0)),
            pl.BlockSpec((None, d, tn), lambda i, j: (layer, 0, j + off)),
        ] + [spec(tm, tn) for spec in extra_specs],
        out_specs=pl.BlockSpec((tm, tn), lambda i, j: (i, j)),
        out_shape=jax.ShapeDtypeStruct((n, n_cols), out_dtype),
        compiler_params=_params("parallel", "parallel"),
        name=name,
    )(h, w_in, *extras)


def _gattn_kernel(q_ref, kt_ref, v_ref, o_ref, qs_ref, m_ref, l_ref, acc_ref, *, tq, tk, n_lat, n_ctx):
    for h in range(GQA_REP):
        qs_ref[h * tq:(h + 1) * tq, :] = q_ref[:, h * HEAD_DIM:(h + 1) * HEAD_DIM]
    m_ref[...] = jnp.full(m_ref.shape, MASK_VALUE, F32)
    l_ref[...] = jnp.zeros(l_ref.shape, F32)
    acc_ref[...] = jnp.zeros(acc_ref.shape, F32)

    def step(kt, v):
        ncol = v.shape[0] // HEAD_DIM
        s = _dot(qs_ref[...], kt)
        m_old = m_ref[...]
        m_new = jnp.maximum(m_old, jnp.max(s, axis=1, keepdims=True))
        alpha = jnp.exp2(m_old - m_new)
        p = jnp.exp2(s - jnp.concatenate([m_new] * ncol, axis=1))
        psum = p[:, :HEAD_DIM]
        for j in range(1, ncol):
            psum = psum + p[:, j * HEAD_DIM:(j + 1) * HEAD_DIM]
        l_ref[...] = alpha * l_ref[...] + psum
        acc_ref[...] = alpha * acc_ref[...] + _dot(p.astype(BF16), v)
        m_ref[...] = m_new

    def body(c, carry):
        off = pl.multiple_of(c * tk, tk)
        step(kt_ref[:, pl.ds(off, tk)], v_ref[pl.ds(off, tk), :])
        return carry

    n_chunks = (n_lat + n_ctx) // tk
    is_ctx = pl.program_id(1) * tq >= n_lat

    @pl.when(jnp.logical_not(is_ctx))
    def _():
        lax.fori_loop(0, n_chunks, body, 0, unroll=n_chunks if n_chunks <= GATTN_MAX_UNROLL else 1)

    @pl.when(is_ctx)
    def _():
        step(kt_ref[:, n_lat:n_lat + n_ctx], v_ref[n_lat:n_lat + n_ctx, :])

    o = acc_ref[...] / jnp.sum(l_ref[...], axis=1, keepdims=True)
    for h in range(GQA_REP):
        o_ref[:, h * HEAD_DIM:(h + 1) * HEAD_DIM] = o[h * tq:(h + 1) * tq].astype(o_ref.dtype)


def _global_attn(qa, ka_t, kv_a, n_lat, n_ctx):
    n = qa.shape[0]
    tq = _tile(n_ctx, 256, 8)
    assert n_lat % tq == 0 and n_ctx % tq == 0
    tk = _tile(n, GATTN_KV_CHUNK, V7X_MXU_COLS)
    gw = GQA_REP * HEAD_DIM
    kern = functools.partial(_gattn_kernel, tq=tq, tk=tk, n_lat=n_lat, n_ctx=n_ctx)
    return pl.pallas_call(
        kern,
        grid=(N_KV_A, n // tq),
        in_specs=[
            pl.BlockSpec((tq, gw), lambda g, i: (i, g)),
            pl.BlockSpec((HEAD_DIM, n), lambda g, i: (g, 0)),
            pl.BlockSpec((n, HEAD_DIM), lambda g, i: (0, N_KV_A + g)),
        ],
        out_specs=pl.BlockSpec((tq, gw), lambda g, i: (i, g)),
        out_shape=jax.ShapeDtypeStruct((n, W_A_Q), BF16),
        scratch_shapes=[
            pltpu.VMEM((GQA_REP * tq, HEAD_DIM), BF16),
            pltpu.VMEM((GQA_REP * tq, HEAD_DIM), F32),
            pltpu.VMEM((GQA_REP * tq, HEAD_DIM), F32),
            pltpu.VMEM((GQA_REP * tq, HEAD_DIM), F32),
        ],
        compiler_params=_params("parallel", "parallel"),
        name="global_attn",
    )(qa, ka_t, kv_a)


def _nbr_kernel(q_ref, *rest):
    k_refs = rest[:NBR_KV_BLOCKS]
    v_refs = rest[NBR_KV_BLOCKS:2 * NBR_KV_BLOCKS]
    kc_ref, vc_ref, bias_ref, o_ref = rest[2 * NBR_KV_BLOCKS:]
    nt = (((1,), (1,)), ((), ()))
    heads = [slice(h * HEAD_DIM, (h + 1) * HEAD_DIM) for h in range(N_HEADS_B)]
    scores = []
    for h, sl in enumerate(heads):
        k = jnp.concatenate([r[:, sl] for r in k_refs] + [kc_ref[:, sl]], axis=0)
        scores.append(lax.dot_general(q_ref[:, sl], k, nt, preferred_element_type=F32) + bias_ref[h])
    probs, sums = [], []
    for s in scores:
        p = jnp.exp2(s - jnp.max(s, axis=1, keepdims=True))
        sums.append(jnp.sum(p, axis=1, keepdims=True))
        probs.append(p.astype(BF16))
    for h, sl in enumerate(heads):
        v = jnp.concatenate([r[:, sl] for r in v_refs] + [vc_ref[:, sl]], axis=0)
        o_ref[:, sl] = (_dot(probs[h], v) / sums[h]).astype(o_ref.dtype)


def _nbr_bias_tables(rpb, n_lat, n_ctx):
    rows = n_lat // GRID_W
    nblk = n_lat // NBR_Q
    kh = min(WIN_H, rows)
    q_rows, kv_rows = NBR_Q // GRID_W, NBR_KEYS // GRID_W
    n_off_r, n_off_c = 2 * WIN_H - 1, 2 * WIN_W - 1
    blocks = (0, 1, 2, nblk - 2, nblk - 1)
    sel_r = np.zeros((len(blocks), n_off_r, q_rows, kv_rows), np.float32)
    for p, blk in enumerate(blocks):
        blk0 = min(max(blk - 2, 0), nblk - NBR_KV_BLOCKS)
        for qr in range(q_rows):
            rq = q_rows * blk + qr
            r0 = min(max(rq - kh // 2, 0), rows - kh)
            for kr in range(kv_rows):
                key_row = q_rows * blk0 + kr
                if r0 <= key_row < r0 + kh:
                    sel_r[p, key_row - rq + WIN_H - 1, qr, kr] = 1.0
    sel_c = np.zeros((n_off_c, GRID_W, GRID_W), np.float32)
    for qc in range(GRID_W):
        c0 = min(max(qc - WIN_W // 2, 0), GRID_W - WIN_W)
        for kc in range(c0, c0 + WIN_W):
            sel_c[kc - qc + WIN_W - 1, qc, kc] = 1.0
    inside = np.einsum("paqk,bcd->pqckd", sel_r, sel_c).reshape(len(blocks), NBR_Q, NBR_KEYS) > 0.5
    bias = jnp.einsum("lhab,paqk,bcd->lphqckd", rpb.astype(F32), sel_r, sel_c,
                      precision=lax.Precision.HIGHEST)
    bias = bias.reshape(rpb.shape[0], len(blocks), N_HEADS_B, NBR_Q, NBR_KEYS) * LOG2_E
    bias = jnp.where(inside[None, :, None], bias, MASK_VALUE)
    bias = jnp.concatenate([bias, jnp.full_like(bias[:, :1], MASK_VALUE)], axis=1)
    return jnp.pad(bias, ((0, 0),) * 4 + ((0, n_ctx),))


def _nbr_attn(qkv_b, bias, layer, n_lat, n_ctx):
    n = qkv_b.shape[0]
    nblk = n_lat // NBR_Q
    assert nblk >= NBR_KV_BLOCKS + 1 and n_ctx % NBR_Q == 0
    ctx_blk = n_lat // n_ctx
    assert n_lat % n_ctx == 0

    def first_kv(i):
        return jnp.clip(i - 2, 0, nblk - NBR_KV_BLOCKS)

    def pattern(i):
        return jnp.where(i >= nblk, NBR_PATTERNS - 1, i - first_kv(i))

    def kv_spec(col, b):
        return pl.BlockSpec((NBR_Q, W_B), lambda i: (first_kv(i) + b, col))

    in_specs = ([pl.BlockSpec((NBR_Q, W_B), lambda i: (i, 0))]
                + [kv_spec(1, b) for b in range(NBR_KV_BLOCKS)]
                + [kv_spec(2, b) for b in range(NBR_KV_BLOCKS)]
                + [pl.BlockSpec((n_ctx, W_B), lambda i: (ctx_blk, 1)),
                   pl.BlockSpec((n_ctx, W_B), lambda i: (ctx_blk, 2)),
                   pl.BlockSpec((None, None, N_HEADS_B, NBR_Q, NBR_KEYS + n_ctx),
                                lambda i: (layer, pattern(i), 0, 0, 0))])
    return pl.pallas_call(
        _nbr_kernel,
        grid=(n // NBR_Q,),
        in_specs=in_specs,
        out_specs=pl.BlockSpec((NBR_Q, W_B), lambda i: (i, 0)),
        out_shape=jax.ShapeDtypeStruct((n, W_B), BF16),
        compiler_params=_params("parallel"),
        name="nbr_attn",
    )(*([qkv_b] * (3 + 2 * NBR_KV_BLOCKS)), bias)


def _sgu_kernel(u_ref, vn_ref, w_ref, bt_ref, o_ref, *, n_chunks):
    for g in range(N_GROUPS_C):
        sl = slice(g * HEAD_DIM, (g + 1) * HEAD_DIM)
        w = w_ref[g]
        b = bt_ref[:, g:g + 1]
        for c in range(n_chunks):
            rs = slice(c * CHUNK, (c + 1) * CHUNK)
            sv = _dot(w, vn_ref[rs, sl]) + b
            o_ref[rs, sl] = (u_ref[rs, sl] * sv).astype(o_ref.dtype)


def _sgu(u, vn, w_s, b_t, layer):
    n = u.shape[0]
    tr = _tile(n, 1280, CHUNK)
    kern = functools.partial(_sgu_kernel, n_chunks=tr // CHUNK)
    return pl.pallas_call(
        kern,
        grid=(n // tr,),
        in_specs=[
            pl.BlockSpec((tr, W_C), lambda i: (i, 0)),
            pl.BlockSpec((tr, W_C), lambda i: (i, 0)),
            pl.BlockSpec((None, N_GROUPS_C, CHUNK, CHUNK), lambda i: (layer, 0, 0, 0)),
            pl.BlockSpec((None, CHUNK, N_GROUPS_C), lambda i: (layer, 0, 0)),
        ],
        out_specs=pl.BlockSpec((tr, W_C), lambda i: (i, 0)),
        out_shape=jax.ShapeDtypeStruct((n, W_C), BF16),
        compiler_params=_params("parallel"),
        name="sgu",
    )(u, vn, w_s, b_t)


def _out_proj_kernel(oa_ref, ob_ref, oc_ref, wa_ref, wb_ref, wc_ref, x_ref, gate_ref, o_ref, *, tm, n_lat):
    y = _dot(oa_ref[...], wa_ref[...]) + _dot(ob_ref[...], wb_ref[...]) + _dot(oc_ref[...], wc_ref[...])
    o_ref[...] = x_ref[...] + _row_select(gate_ref, tm, n_lat) * y


def _out_proj(oa, ob, oc, w_out, x, mods, layer, n_lat):
    n, d = x.shape
    tm = _tile(n, 1280, 128)
    tn = _tile(d, 512, 128)
    assert W_A_Q % W_B == 0 and W_B == W_C
    kern = functools.partial(_out_proj_kernel, tm=tm, n_lat=n_lat)
    return pl.pallas_call(
        kern,
        grid=(n // tm, d // tn),
        in_specs=[
            pl.BlockSpec((tm, W_A_Q), lambda i, j: (i, 0)),
            pl.BlockSpec((tm, W_B), lambda i, j: (i, 0)),
            pl.BlockSpec((tm, W_C), lambda i, j: (i, 0)),
            pl.BlockSpec((None, W_A_Q, tn), lambda i, j: (layer, 0, j)),
            pl.BlockSpec((None, W_B, tn), lambda i, j: (layer, W_A_Q // W_B, j)),
            pl.BlockSpec((None, W_C, tn), lambda i, j: (layer, W_A_Q // W_B + 1, j)),
            pl.BlockSpec((tm, tn), lambda i, j: (i, j)),
            pl.BlockSpec((None, None, MOD_ROWS, tn), lambda i, j: (layer, 5, 0, j)),
        ],
        out_specs=pl.BlockSpec((tm, tn), lambda i, j: (i, j)),
        out_shape=jax.ShapeDtypeStruct((n, d), F32),
        compiler_params=_params("parallel", "parallel"),
        name="out_proj",
    )(oa, ob, oc, w_out, w_out, w_out, x, mods)


def _rope_tables(n_lat, n_ctx):
    t = jnp.arange(n_lat, dtype=jnp.int32)
    row = (t // GRID_W).astype(F32)
    col = (t % GRID_W).astype(F32)
    half = HEAD_DIM // 2
    inv = ROPE_THETA ** (-jnp.arange(0, half, 2, dtype=F32) / half)
    ang_r = row[:, None] * inv[None, :]
    ang_c = col[:, None] * inv[None, :]
    ang = jnp.concatenate([ang_r, ang_r, ang_c, ang_c], axis=-1)
    lane = jnp.arange(HEAD_DIM)
    sign = jnp.where((lane % half) < HEAD_DIM // 4, -1.0, 1.0).astype(F32)
    cos = jnp.concatenate([jnp.cos(ang), jnp.ones((n_ctx, HEAD_DIM), F32)], axis=0)
    sin = jnp.concatenate([jnp.sin(ang) * sign, jnp.zeros((n_ctx, HEAD_DIM), F32)], axis=0)
    return cos, sin


def _mixer_groups(h, l, mix_w_in_b, qk_g, sgu_g, cos, sin, nbr_bias, sgu_w_b, sgu_b_t, n_lat, n_ctx):
    rope_specs = [lambda tm, tn: pl.BlockSpec((tm, HEAD_DIM), lambda i, j: (i, 0))] * 2

    def g_spec(which):
        return lambda tm, tn: pl.BlockSpec((None, None, 1, HEAD_DIM), lambda i, j: (l, which, 0, 0))

    col_ka = W_A_Q
    col_b = col_ka + 2 * W_A_KV
    col_u = col_b + 3 * W_B
    col_v = col_u + W_C
    qa = _proj(h, mix_w_in_b, l, 0, W_A_Q,
               functools.partial(_epi_qk_rope, out_scale=HEAD_DIM ** -0.5 * LOG2_E),
               [qk_g, cos, sin], [g_spec(0)] + rope_specs, BF16, "proj_qa", chunk=V7X_MXU_COLS)
    kv_a = _proj(h, mix_w_in_b, l, col_ka, 2 * W_A_KV, _epi_kv_a,
                 [qk_g, cos, sin], [g_spec(1)] + rope_specs, BF16, "proj_kv_a", chunk=V7X_MXU_COLS)
    qkv_b = _proj(h, mix_w_in_b, l, col_b, 3 * W_B, _epi_qkv_b, [], [], BF16, "proj_b", tn=W_B)
    u = _proj(h, mix_w_in_b, l, col_u, W_C, _epi_gelu, [], [], F32, "proj_u", chunk=V7X_MXU_COLS)
    vn = _proj(h, mix_w_in_b, l, col_v, W_C, _epi_gelu_rms, [sgu_g],
               [lambda tm, tn: pl.BlockSpec((None, 1, tn), lambda i, j: (l, 0, j))], BF16, "proj_v",
               chunk=V7X_MXU_COLS)
    oa = _global_attn(qa, kv_a[:, :W_A_KV].T, kv_a, n_lat, n_ctx)
    ob = _nbr_attn(qkv_b, nbr_bias, l, n_lat, n_ctx)
    oc = _sgu(u, vn, sgu_w_b, sgu_b_t, l)
    return oa, ob, oc


def kernel(x, c, ctx, c_ctx, norm_g, mod_w_down, mod_w_up, mod_b, ffn_w_in, ffn_w_out, mix_w_in, mix_w_out,
           qk_norm_g, rpb, sgu_norm_g, sgu_w, sgu_b, final_norm_g):
    batch, n_lat, d = x.shape
    n_ctx = ctx.shape[1]
    depth = norm_g.shape[0]
    assert batch == 1 and c.shape[0] == 1

    ffn_w_out_b = ffn_w_out.astype(BF16)
    mix_w_in_b = mix_w_in.astype(BF16)
    mix_w_out_b = mix_w_out.astype(BF16)
    sgu_w_b = sgu_w.astype(BF16)
    sgu_b_t = jnp.swapaxes(sgu_b, 1, 2)
    norm_g4 = norm_g.reshape(depth, 3, 1, d)

    cond = jnp.zeros((MOD_ROWS, d), F32).at[0].set(c[0]).at[1].set(c_ctx)
    mods = _adaln(cond, mod_w_down, mod_w_up, mod_b)
    mods = mods.reshape(depth, MOD_ROWS, N_MOD, d).swapaxes(1, 2)

    cos, sin = _rope_tables(n_lat, n_ctx)
    nbr_bias = _nbr_bias_tables(rpb, n_lat, n_ctx)
    qk_g = qk_norm_g.reshape(depth, 2, 1, HEAD_DIM)
    sgu_g = sgu_norm_g.reshape(depth, 1, W_C)

    xs = jnp.concatenate([x[0], ctx[0]], axis=0)

    for l in range(depth):
        h = _norm_mod(xs, norm_g4, mods, l, 0, n_lat)
        xs = _ffn_down(_ffn_up(h, ffn_w_in, l, 0), ffn_w_out_b, xs, mods, l, 0, 2, n_lat)

        h = _norm_mod(xs, norm_g4, mods, l, 1, n_lat)
        oa, ob, oc = _mixer_groups(h, l, mix_w_in_b, qk_g, sgu_g, cos, sin, nbr_bias, sgu_w_b, sgu_b_t,
                                   n_lat, n_ctx)
        xs = _out_proj(oa, ob, oc, mix_w_out_b, xs, mods, l, n_lat)

        h = _norm_mod(xs, norm_g4, mods, l, 2, n_lat)
        xs = _ffn_down(_ffn_up(h, ffn_w_in, l, 1), ffn_w_out_b, xs, mods, l, 1, 8, n_lat)

    return _final_norm(xs, final_norm_g, n_lat).reshape(1, n_lat, d)
```

```python
import functools
import math

import jax
import jax.numpy as jnp
import numpy as np
from jax import lax
from jax.experimental import pallas as pl
from jax.experimental.pallas import tpu as pltpu

HEAD_DIM = 128
GRID_W = 64
N_HEADS_A = 16
N_KV_A = 4
N_HEADS_B = 8
N_GROUPS_C = 8
WIN_H = 8
WIN_W = 16
CHUNK = 128
N_MOD = 9
ROPE_THETA = 10000.0
EPS = 1e-6
W_A_Q = N_HEADS_A * HEAD_DIM
W_A_KV = N_KV_A * HEAD_DIM
W_B = N_HEADS_B * HEAD_DIM
W_C = N_GROUPS_C * HEAD_DIM
GQA_REP = N_HEADS_A // N_KV_A

NBR_Q = 2 * GRID_W
NBR_KV_BLOCKS = 5
NBR_KEYS = NBR_KV_BLOCKS * NBR_Q
NBR_PATTERNS = 6
MASK_VALUE = -1e30
LOG2_E = 1.4426950408889634

V7X_VMEM_BYTES = 64 * 1024 * 1024
VMEM_LIMIT_BYTES = V7X_VMEM_BYTES - 8 * 1024 * 1024
V7X_MXU_COLS = 256
LANES = 128
GATTN_Q_ROWS = 256
GATTN_KV_CHUNK = 3584
GATTN_MAX_UNROLL = 8
MOD_ROWS = 8
BF16_SUBLANES = 16
NORM_ROWS = BF16_SUBLANES

TILE_FFN_UP = (1280, V7X_MXU_COLS)
TILE_FFN_DOWN = (1040, 512)
TILE_PROJ = (1040, 1024)
TILE_OUT_PROJ = (1664, 512)
TILE_NORM_ROWS = 256
TILE_SGU_ROWS = 1280
TILE_ADALN_COLS = 4096

F32 = jnp.float32
BF16 = jnp.bfloat16


def _tile(n, target, align):
    best = None
    for t in range(align, min(n, target) + 1, align):
        if n % t == 0:
            best = t
    assert best is not None, (n, target, align)
    return best


def _params(*sem):
    return pltpu.CompilerParams(dimension_semantics=sem, vmem_limit_bytes=VMEM_LIMIT_BYTES)


def _row_select(mod_ref, tm, n_lat):
    row = pl.program_id(0) * tm + lax.broadcasted_iota(jnp.int32, (tm, 1), 0)
    return jnp.where(row >= n_lat, mod_ref[1:2, :], mod_ref[0:1, :])


def _dot(a, b):
    return jnp.dot(a, b, preferred_element_type=F32)


def _adaln_kernel(cond_ref, wd_ref, wu_ref, b_ref, o_ref, mid_ref):
    @pl.when(pl.program_id(1) == 0)
    def _():
        s = cond_ref[...]
        s = s * (1.0 / (1.0 + jnp.exp(-s)))
        mid_ref[...] = jnp.dot(s, wd_ref[...], preferred_element_type=F32, precision=lax.Precision.HIGHEST)

    o_ref[...] = jnp.dot(mid_ref[...], wu_ref[...], preferred_element_type=F32,
                         precision=lax.Precision.HIGHEST) + b_ref[...]


def _adaln(cond, w_down, w_up, b_up):
    depth, d, r = w_down.shape
    n = w_up.shape[-1]
    tn = _tile(n, TILE_ADALN_COLS, LANES)
    return pl.pallas_call(
        _adaln_kernel,
        grid=(depth, n // tn),
        in_specs=[
            pl.BlockSpec((MOD_ROWS, d), lambda l, j: (0, 0)),
            pl.BlockSpec((None, d, r), lambda l, j: (l, 0, 0)),
            pl.BlockSpec((None, r, tn), lambda l, j: (l, 0, j)),
            pl.BlockSpec((None, 1, tn), lambda l, j: (l, 0, j)),
        ],
        out_specs=pl.BlockSpec((None, MOD_ROWS, tn), lambda l, j: (l, 0, j)),
        out_shape=jax.ShapeDtypeStruct((depth, MOD_ROWS, n), F32),
        scratch_shapes=[pltpu.VMEM((MOD_ROWS, r), F32)],
        compiler_params=_params("parallel", "arbitrary"),
        name="adaln",
    )(cond, w_down, w_up, b_up.reshape(depth, 1, n))


def _norm_mod_kernel(x_ref, g_ref, shift_ref, scale_ref, o_ref, *, tr, n_lat):
    which = (pl.program_id(0) * tr >= n_lat).astype(jnp.int32)
    gain = g_ref[...] * (1.0 + scale_ref[pl.ds(which, 1), :])
    shift = shift_ref[pl.ds(which, 1), :]

    def body(r, carry):
        rs = pl.ds(pl.multiple_of(r * NORM_ROWS, NORM_ROWS), NORM_ROWS)
        x = x_ref[rs, :]
        inv = lax.rsqrt(jnp.mean(x * x, axis=-1, keepdims=True) + EPS)
        o_ref[rs, :] = (x_ref[rs, :] * inv * gain + shift).astype(o_ref.dtype)
        return carry

    lax.fori_loop(0, tr // NORM_ROWS, body, 0, unroll=2)


def _norm_mod(x, norm_g, mods, layer, sub, n_lat):
    n, d = x.shape
    tr = _tile(math.gcd(n_lat, n - n_lat), TILE_NORM_ROWS, NORM_ROWS)
    kern = functools.partial(_norm_mod_kernel, tr=tr, n_lat=n_lat)
    return pl.pallas_call(
        kern,
        grid=(n // tr,),
        in_specs=[
            pl.BlockSpec((tr, d), lambda i: (i, 0)),
            pl.BlockSpec((None, None, 1, d), lambda i: (layer, sub, 0, 0)),
            pl.BlockSpec((None, None, MOD_ROWS, d), lambda i: (layer, 3 * sub, 0, 0)),
            pl.BlockSpec((None, None, MOD_ROWS, d), lambda i: (layer, 3 * sub + 1, 0, 0)),
        ],
        out_specs=pl.BlockSpec((tr, d), lambda i: (i, 0)),
        out_shape=jax.ShapeDtypeStruct((n, d), BF16),
        compiler_params=_params("parallel"),
        name="norm_mod",
    )(x, norm_g, mods, mods)


def _final_norm_kernel(x_ref, g_ref, o_ref):
    g = g_ref[...]

    def body(r, carry):
        rs = pl.ds(pl.multiple_of(r * NORM_ROWS, NORM_ROWS), NORM_ROWS)
        x = x_ref[rs, :]
        inv = lax.rsqrt(jnp.mean(x * x, axis=-1, keepdims=True) + EPS)
        o_ref[rs, :] = x_ref[rs, :] * inv * g
        return carry

    lax.fori_loop(0, x_ref.shape[0] // NORM_ROWS, body, 0, unroll=2)


def _final_norm(x, g, n_lat):
    d = x.shape[1]
    tr = _tile(n_lat, TILE_NORM_ROWS, NORM_ROWS)
    return pl.pallas_call(
        _final_norm_kernel,
        grid=(n_lat // tr,),
        in_specs=[pl.BlockSpec((tr, d), lambda i: (i, 0)),
                  pl.BlockSpec((1, d), lambda i: (0, 0))],
        out_specs=pl.BlockSpec((tr, d), lambda i: (i, 0)),
        out_shape=jax.ShapeDtypeStruct((n_lat, d), F32),
        compiler_params=_params("parallel"),
        name="final_norm",
    )(x, g.reshape(1, d))


def _ffn_up_kernel(h_ref, wa_ref, wb_ref, o_ref):
    h = h_ref[...]
    a = _dot(h, wa_ref[...].astype(BF16))
    b = _dot(h, wb_ref[...].astype(BF16))
    o_ref[...] = (a * (1.0 / (1.0 + jnp.exp(-a))) * b).astype(o_ref.dtype)


def _ffn_up(h, w_in, layer, half):
    n, d = h.shape
    f = w_in.shape[-1] // 2
    tm = _tile(n, TILE_FFN_UP[0], BF16_SUBLANES)
    tn = _tile(f, TILE_FFN_UP[1], LANES)
    nb = f // tn
    return pl.pallas_call(
        _ffn_up_kernel,
        grid=(n // tm, nb),
        in_specs=[
            pl.BlockSpec((tm, d), lambda i, j: (i, 0)),
            pl.BlockSpec((None, None, d, tn), lambda i, j: (layer, half, 0, j)),
            pl.BlockSpec((None, None, d, tn), lambda i, j: (layer, half, 0, j + nb)),
        ],
        out_specs=pl.BlockSpec((tm, tn), lambda i, j: (i, j)),
        out_shape=jax.ShapeDtypeStruct((n, f), BF16),
        compiler_params=_params("parallel", "parallel"),
        name="ffn_up",
    )(h, w_in, w_in)


def _ffn_down_kernel(g_ref, w_ref, x_ref, gate_ref, o_ref, *, tm, n_lat):
    y = _dot(g_ref[...], w_ref[...])
    o_ref[...] = x_ref[...] + 0.5 * _row_select(gate_ref, tm, n_lat) * y


def _ffn_down(g, w_out, x, mods, layer, half, gate_idx, n_lat):
    n, f = g.shape
    d = x.shape[1]
    tm = _tile(n, TILE_FFN_DOWN[0], BF16_SUBLANES)
    tn = _tile(d, TILE_FFN_DOWN[1], LANES)
    kern = functools.partial(_ffn_down_kernel, tm=tm, n_lat=n_lat)
    return pl.pallas_call(
        kern,
        grid=(n // tm, d // tn),
        in_specs=[
            pl.BlockSpec((tm, f), lambda i, j: (i, 0)),
            pl.BlockSpec((None, None, f, tn), lambda i, j: (layer, half, 0, j)),
            pl.BlockSpec((tm, tn), lambda i, j: (i, j)),
            pl.BlockSpec((None, None, MOD_ROWS, tn), lambda i, j: (layer, gate_idx, 0, j)),
        ],
        out_specs=pl.BlockSpec((tm, tn), lambda i, j: (i, j)),
        out_shape=jax.ShapeDtypeStruct((n, d), F32),
        compiler_params=_params("parallel", "parallel"),
        name="ffn_down",
    )(g, w_out, x, mods)


def _head_rms(x, g):
    return x * lax.rsqrt(jnp.mean(x * x, axis=-1, keepdims=True) + EPS) * g


def _rope(x, cos, sin_signed):
    lane = lax.broadcasted_iota(jnp.int32, x.shape, 1)
    low = jnp.bitwise_and(lane, HEAD_DIM // 2 - 1) < (HEAD_DIM // 4)
    rot = jnp.where(low, pltpu.roll(x, HEAD_DIM - HEAD_DIM // 4, 1), pltpu.roll(x, HEAD_DIM // 4, 1))
    return x * cos + rot * sin_signed


def _epi_qk_rope(acc, cols, g_ref, cos_ref, sin_ref, *, out_scale):
    g = g_ref[...]
    cos = cos_ref[...]
    sin = sin_ref[...]
    heads = []
    for h in range(acc.shape[1] // HEAD_DIM):
        x = _head_rms(acc[:, h * HEAD_DIM:(h + 1) * HEAD_DIM], g)
        heads.append(_rope(x, cos, sin) * out_scale)
    return jnp.concatenate(heads, axis=1)


def _epi_plain(acc, cols):
    return acc


def _epi_qkv_b(acc, cols):
    return acc * jnp.where(pl.program_id(1) == 0, HEAD_DIM ** -0.5 * LOG2_E, 1.0)


def _epi_kv_a(acc, cols, g_ref, cos_ref, sin_ref):
    if cols.start < W_A_KV:
        return _epi_qk_rope(acc, cols, g_ref, cos_ref, sin_ref, out_scale=1.0)
    return acc


def _gelu_tanh(x):
    return 0.5 * x * (1.0 + jnp.tanh(0.7978845608028654 * (x + 0.044715 * (x * x * x))))


def _epi_gelu(acc, cols):
    return _gelu_tanh(acc)


def _epi_gelu_rms(acc, cols, g_ref):
    y = _gelu_tanh(acc)
    g = g_ref[:, cols]
    groups = []
    for c in range(acc.shape[1] // HEAD_DIM):
        sl = slice(c * HEAD_DIM, (c + 1) * HEAD_DIM)
        groups.append(_head_rms(y[:, sl], g[:, sl]))
    return jnp.concatenate(groups, axis=1)


def _proj_kernel(h_ref, w_ref, *rest, epilogue, chunk):
    extras, o_ref = rest[:-1], rest[-1]
    h = h_ref[...]
    for c in range(o_ref.shape[1] // chunk):
        cols = slice(c * chunk, (c + 1) * chunk)
        o_ref[:, cols] = epilogue(_dot(h, w_ref[:, cols]), cols, *extras).astype(o_ref.dtype)


def _proj(h, w_in, layer, col_off, n_cols, epilogue, extras, extra_specs, out_dtype, name, chunk=None, tn=None):
    n, d = h.shape
    tm = _tile(n, TILE_PROJ[0], BF16_SUBLANES)
    tn = _tile(n_cols, TILE_PROJ[1], LANES) if tn is None else tn
    chunk = tn if chunk is None else chunk
    assert col_off % tn == 0 and tn % chunk == 0
    off = col_off // tn
    return pl.pallas_call(
        functools.partial(_proj_kernel, epilogue=epilogue, chunk=chunk),
        grid=(n // tm, n_cols // tn),
        in_specs=[
            pl.BlockSpec((tm, d), lambda i, j: (i, 0)),
            pl.BlockSpec((None, d, tn), lambda i, j: (layer, 0, j + off)),
        ] + [spec(tm, tn) for spec in extra_specs],
        out_specs=pl.BlockSpec((tm, tn), lambda i, j: (i, j)),
        out_shape=jax.ShapeDtypeStruct((n, n_cols), out_dtype),
        compiler_params=_params("parallel", "parallel"),
        name=name,
    )(h, w_in, *extras)


def _gattn_kernel(q_ref, kt_ref, v_ref, o_ref, qs_ref, m_ref, l_ref, acc_ref, *, tq, tk, n_lat, n_ctx):
    for h in range(GQA_REP):
        qs_ref[h * tq:(h + 1) * tq, :] = q_ref[:, h * HEAD_DIM:(h + 1) * HEAD_DIM]
    m_ref[...] = jnp.full(m_ref.shape, MASK_VALUE, F32)
    l_ref[...] = jnp.zeros(l_ref.shape, F32)
    acc_ref[...] = jnp.zeros(acc_ref.shape, F32)

    def step(kt, v):
        ncol = v.shape[0] // HEAD_DIM
        s = _dot(qs_ref[...], kt)
        m_old = m_ref[...]
        m_new = jnp.maximum(m_old, jnp.max(s, axis=1, keepdims=True))
        alpha = jnp.exp2(m_old - m_new)
        p = jnp.exp2(s - jnp.concatenate([m_new] * ncol, axis=1))
        psum = p[:, :HEAD_DIM]
        for j in range(1, ncol):
            psum = psum + p[:, j * HEAD_DIM:(j + 1) * HEAD_DIM]
        l_ref[...] = alpha * l_ref[...] + psum
        acc_ref[...] = alpha * acc_ref[...] + _dot(p.astype(BF16), v)
        m_ref[...] = m_new

    def body(c, carry):
        off = pl.multiple_of(c * tk, tk)
        step(kt_ref[:, pl.ds(off, tk)], v_ref[pl.ds(off, tk), :])
        return carry

    n_chunks = (n_lat + n_ctx) // tk
    is_ctx = pl.program_id(1) * tq >= n_lat

    @pl.when(jnp.logical_not(is_ctx))
    def _():
        lax.fori_loop(0, n_chunks, body, 0, unroll=n_chunks if n_chunks <= GATTN_MAX_UNROLL else 1)

    @pl.when(is_ctx)
    def _():
        step(kt_ref[:, n_lat:n_lat + n_ctx], v_ref[n_lat:n_lat + n_ctx, :])

    o = acc_ref[...] / jnp.sum(l_ref[...], axis=1, keepdims=True)
    for h in range(GQA_REP):
        o_ref[:, h * HEAD_DIM:(h + 1) * HEAD_DIM] = o[h * tq:(h + 1) * tq].astype(o_ref.dtype)


def _global_attn(qa, ka_t, kv_a, n_lat, n_ctx):
    n = qa.shape[0]
    tq = _tile(n_ctx, GATTN_Q_ROWS, BF16_SUBLANES)
    assert n_lat % tq == 0 and n_ctx % tq == 0
    tk = _tile(n, GATTN_KV_CHUNK, V7X_MXU_COLS)
    gw = GQA_REP * HEAD_DIM
    kern = functools.partial(_gattn_kernel, tq=tq, tk=tk, n_lat=n_lat, n_ctx=n_ctx)
    return pl.pallas_call(
        kern,
        grid=(N_KV_A, n // tq),
        in_specs=[
            pl.BlockSpec((tq, gw), lambda g, i: (i, g)),
            pl.BlockSpec((HEAD_DIM, n), lambda g, i: (g, 0)),
            pl.BlockSpec((n, HEAD_DIM), lambda g, i: (0, N_KV_A + g)),
        ],
        out_specs=pl.BlockSpec((tq, gw), lambda g, i: (i, g)),
        out_shape=jax.ShapeDtypeStruct((n, W_A_Q), BF16),
        scratch_shapes=[
            pltpu.VMEM((GQA_REP * tq, HEAD_DIM), BF16),
            pltpu.VMEM((GQA_REP * tq, HEAD_DIM), F32),
            pltpu.VMEM((GQA_REP * tq, HEAD_DIM), F32),
            pltpu.VMEM((GQA_REP * tq, HEAD_DIM), F32),
        ],
        compiler_params=_params("parallel", "parallel"),
        name="global_attn",
    )(qa, ka_t, kv_a)


def _nbr_kernel(q_ref, *rest):
    k_refs = rest[:NBR_KV_BLOCKS]
    v_refs = rest[NBR_KV_BLOCKS:2 * NBR_KV_BLOCKS]
    kc_ref, vc_ref, bias_ref, o_ref = rest[2 * NBR_KV_BLOCKS:]
    nt = (((1,), (1,)), ((), ()))
    heads = [slice(h * HEAD_DIM, (h + 1) * HEAD_DIM) for h in range(N_HEADS_B)]
    scores = []
    for h, sl in enumerate(heads):
        k = jnp.concatenate([r[:, sl] for r in k_refs] + [kc_ref[:, sl]], axis=0)
        scores.append(lax.dot_general(q_ref[:, sl], k, nt, preferred_element_type=F32) + bias_ref[h])
    probs, sums = [], []
    for s in scores:
        p = jnp.exp2(s - jnp.max(s, axis=1, keepdims=True))
        sums.append(jnp.sum(p, axis=1, keepdims=True))
        probs.append(p.astype(BF16))
    for h, sl in enumerate(heads):
        v = jnp.concatenate([r[:, sl] for r in v_refs] + [vc_ref[:, sl]], axis=0)
        o_ref[:, sl] = (_dot(probs[h], v) / sums[h]).astype(o_ref.dtype)


def _nbr_bias_tables(rpb, n_lat, n_ctx):
    rows = n_lat // GRID_W
    nblk = n_lat // NBR_Q
    kh = min(WIN_H, rows)
    q_rows, kv_rows = NBR_Q // GRID_W, NBR_KEYS // GRID_W
    n_off_r, n_off_c = 2 * WIN_H - 1, 2 * WIN_W - 1
    blocks = (0, 1, 2, nblk - 2, nblk - 1)
    sel_r = np.zeros((len(blocks), n_off_r, q_rows, kv_rows), np.float32)
    for p, blk in enumerate(blocks):
        blk0 = min(max(blk - 2, 0), nblk - NBR_KV_BLOCKS)
        for qr in range(q_rows):
            rq = q_rows * blk + qr
            r0 = min(max(rq - kh // 2, 0), rows - kh)
            for kr in range(kv_rows):
                key_row = q_rows * blk0 + kr
                if r0 <= key_row < r0 + kh:
                    sel_r[p, key_row - rq + WIN_H - 1, qr, kr] = 1.0
    sel_c = np.zeros((n_off_c, GRID_W, GRID_W), np.float32)
    for qc in range(GRID_W):
        c0 = min(max(qc - WIN_W // 2, 0), GRID_W - WIN_W)
        for kc in range(c0, c0 + WIN_W):
            sel_c[kc - qc + WIN_W - 1, qc, kc] = 1.0
    inside = np.einsum("paqk,bcd->pqckd", sel_r, sel_c).reshape(len(blocks), NBR_Q, NBR_KEYS) > 0.5
    bias = jnp.einsum("lhab,paqk,bcd->lphqckd", rpb.astype(F32), sel_r, sel_c,
                      precision=lax.Precision.HIGHEST)
    bias = bias.reshape(rpb.shape[0], len(blocks), N_HEADS_B, NBR_Q, NBR_KEYS) * LOG2_E
    bias = jnp.where(inside[None, :, None], bias, MASK_VALUE)
    bias = jnp.concatenate([bias, jnp.full_like(bias[:, :1], MASK_VALUE)], axis=1)
    return jnp.pad(bias, ((0, 0),) * 4 + ((0, n_ctx),))


def _nbr_attn(qkv_b, bias, layer, n_lat, n_ctx):
    n = qkv_b.shape[0]
    nblk = n_lat // NBR_Q
    assert nblk >= NBR_KV_BLOCKS + 1 and n_ctx % NBR_Q == 0
    ctx_blk = n_lat // n_ctx
    assert n_lat % n_ctx == 0

    def first_kv(i):
        return jnp.clip(i - 2, 0, nblk - NBR_KV_BLOCKS)

    def pattern(i):
        return jnp.where(i >= nblk, NBR_PATTERNS - 1, i - first_kv(i))

    def kv_spec(col, b):
        return pl.BlockSpec((NBR_Q, W_B), lambda i: (first_kv(i) + b, col))

    in_specs = ([pl.BlockSpec((NBR_Q, W_B), lambda i: (i, 0))]
                + [kv_spec(1, b) for b in range(NBR_KV_BLOCKS)]
                + [kv_spec(2, b) for b in range(NBR_KV_BLOCKS)]
                + [pl.BlockSpec((n_ctx, W_B), lambda i: (ctx_blk, 1)),
                   pl.BlockSpec((n_ctx, W_B), lambda i: (ctx_blk, 2)),
                   pl.BlockSpec((None, None, N_HEADS_B, NBR_Q, NBR_KEYS + n_ctx),
                                lambda i: (layer, pattern(i), 0, 0, 0))])
    return pl.pallas_call(
        _nbr_kernel,
        grid=(n // NBR_Q,),
        in_specs=in_specs,
        out_specs=pl.BlockSpec((NBR_Q, W_B), lambda i: (i, 0)),
        out_shape=jax.ShapeDtypeStruct((n, W_B), BF16),
        compiler_params=_params("parallel"),
        name="nbr_attn",
    )(*([qkv_b] * (3 + 2 * NBR_KV_BLOCKS)), bias)


def _sgu_kernel(u_ref, vn_ref, w_ref, bt_ref, o_ref, *, n_chunks):
    for g in range(N_GROUPS_C):
        sl = slice(g * HEAD_DIM, (g + 1) * HEAD_DIM)
        w = w_ref[g]
        b = bt_ref[:, g:g + 1]
        for c in range(n_chunks):
            rs = slice(c * CHUNK, (c + 1) * CHUNK)
            sv = _dot(w, vn_ref[rs, sl]) + b
            o_ref[rs, sl] = (u_ref[rs, sl] * sv).astype(o_ref.dtype)


def _sgu(u, vn, w_s, b_t, layer):
    n = u.shape[0]
    tr = _tile(n, TILE_SGU_ROWS, CHUNK)
    kern = functools.partial(_sgu_kernel, n_chunks=tr // CHUNK)
    return pl.pallas_call(
        kern,
        grid=(n // tr,),
        in_specs=[
            pl.BlockSpec((tr, W_C), lambda i: (i, 0)),
            pl.BlockSpec((tr, W_C), lambda i: (i, 0)),
            pl.BlockSpec((None, N_GROUPS_C, CHUNK, CHUNK), lambda i: (layer, 0, 0, 0)),
            pl.BlockSpec((None, CHUNK, N_GROUPS_C), lambda i: (layer, 0, 0)),
        ],
        out_specs=pl.BlockSpec((tr, W_C), lambda i: (i, 0)),
        out_shape=jax.ShapeDtypeStruct((n, W_C), BF16),
        compiler_params=_params("parallel"),
        name="sgu",
    )(u, vn, w_s, b_t)


def _out_proj_kernel(oa_ref, ob_ref, oc_ref, wa_ref, wb_ref, wc_ref, x_ref, gate_ref, o_ref, *, tm, n_lat):
    y = _dot(oa_ref[...], wa_ref[...]) + _dot(ob_ref[...], wb_ref[...]) + _dot(oc_ref[...], wc_ref[...])
    o_ref[...] = x_ref[...] + _row_select(gate_ref, tm, n_lat) * y


def _out_proj(oa, ob, oc, w_out, x, mods, layer, n_lat):
    n, d = x.shape
    tm = _tile(n, TILE_OUT_PROJ[0], BF16_SUBLANES)
    tn = _tile(d, TILE_OUT_PROJ[1], LANES)
    assert W_A_Q % W_B == 0 and W_B == W_C
    kern = functools.partial(_out_proj_kernel, tm=tm, n_lat=n_lat)
    return pl.pallas_call(
        kern,
        grid=(n // tm, d // tn),
        in_specs=[
            pl.BlockSpec((tm, W_A_Q), lambda i, j: (i, 0)),
            pl.BlockSpec((tm, W_B), lambda i, j: (i, 0)),
            pl.BlockSpec((tm, W_C), lambda i, j: (i, 0)),
            pl.BlockSpec((None, W_A_Q, tn), lambda i, j: (layer, 0, j)),
            pl.BlockSpec((None, W_B, tn), lambda i, j: (layer, W_A_Q // W_B, j)),
            pl.BlockSpec((None, W_C, tn), lambda i, j: (layer, W_A_Q // W_B + 1, j)),
            pl.BlockSpec((tm, tn), lambda i, j: (i, j)),
            pl.BlockSpec((None, None, MOD_ROWS, tn), lambda i, j: (layer, 5, 0, j)),
        ],
        out_specs=pl.BlockSpec((tm, tn), lambda i, j: (i, j)),
        out_shape=jax.ShapeDtypeStruct((n, d), F32),
        compiler_params=_params("parallel", "parallel"),
        name="out_proj",
    )(oa, ob, oc, w_out, w_out, w_out, x, mods)


def _rope_tables(n_lat, n_ctx):
    t = jnp.arange(n_lat, dtype=jnp.int32)
    row = (t // GRID_W).astype(F32)
    col = (t % GRID_W).astype(F32)
    half = HEAD_DIM // 2
    inv = ROPE_THETA ** (-jnp.arange(0, half, 2, dtype=F32) / half)
    ang_r = row[:, None] * inv[None, :]
    ang_c = col[:, None] * inv[None, :]
    ang = jnp.concatenate([ang_r, ang_r, ang_c, ang_c], axis=-1)
    lane = jnp.arange(HEAD_DIM)
    sign = jnp.where((lane % half) < HEAD_DIM // 4, -1.0, 1.0).astype(F32)
    cos = jnp.concatenate([jnp.cos(ang), jnp.ones((n_ctx, HEAD_DIM), F32)], axis=0)
    sin = jnp.concatenate([jnp.sin(ang) * sign, jnp.zeros((n_ctx, HEAD_DIM), F32)], axis=0)
    return cos, sin


def _mixer_groups(h, l, mix_w_in_b, qk_g, sgu_g, cos, sin, nbr_bias, sgu_w_b, sgu_b_t, n_lat, n_ctx):
    rope_specs = [lambda tm, tn: pl.BlockSpec((tm, HEAD_DIM), lambda i, j: (i, 0))] * 2

    def g_spec(which):
        return lambda tm, tn: pl.BlockSpec((None, None, 1, HEAD_DIM), lambda i, j: (l, which, 0, 0))

    col_ka = W_A_Q
    col_b = col_ka + 2 * W_A_KV
    col_u = col_b + 3 * W_B
    col_v = col_u + W_C
    qa = _proj(h, mix_w_in_b, l, 0, W_A_Q,
               functools.partial(_epi_qk_rope, out_scale=HEAD_DIM ** -0.5 * LOG2_E),
               [qk_g, cos, sin], [g_spec(0)] + rope_specs, BF16, "proj_qa", chunk=V7X_MXU_COLS)
    kv_a = _proj(h, mix_w_in_b, l, col_ka, 2 * W_A_KV, _epi_kv_a,
                 [qk_g, cos, sin], [g_spec(1)] + rope_specs, BF16, "proj_kv_a", chunk=V7X_MXU_COLS)
    qkv_b = _proj(h, mix_w_in_b, l, col_b, 3 * W_B, _epi_qkv_b, [], [], BF16, "proj_b", tn=W_B)
    u = _proj(h, mix_w_in_b, l, col_u, W_C, _epi_gelu, [], [], F32, "proj_u", chunk=V7X_MXU_COLS)
    vn = _proj(h, mix_w_in_b, l, col_v, W_C, _epi_gelu_rms, [sgu_g],
               [lambda tm, tn: pl.BlockSpec((None, 1, tn), lambda i, j: (l, 0, j))], BF16, "proj_v",
               chunk=V7X_MXU_COLS)
    oa = _global_attn(qa, kv_a[:, :W_A_KV].T, kv_a, n_lat, n_ctx)
    ob = _nbr_attn(qkv_b, nbr_bias, l, n_lat, n_ctx)
    oc = _sgu(u, vn, sgu_w_b, sgu_b_t, l)
    return oa, ob, oc


def kernel(x, c, ctx, c_ctx, norm_g, mod_w_down, mod_w_up, mod_b, ffn_w_in, ffn_w_out, mix_w_in, mix_w_out,
           qk_norm_g, rpb, sgu_norm_g, sgu_w, sgu_b, final_norm_g):
    batch, n_lat, d = x.shape
    n_ctx = ctx.shape[1]
    depth = norm_g.shape[0]
    assert batch == 1 and c.shape[0] == 1

    ffn_w_out_b = ffn_w_out.astype(BF16)
    mix_w_in_b = mix_w_in.astype(BF16)
    mix_w_out_b = mix_w_out.astype(BF16)
    sgu_w_b = sgu_w.astype(BF16)
    sgu_b_t = jnp.swapaxes(sgu_b, 1, 2)
    norm_g4 = norm_g.reshape(depth, 3, 1, d)

    cond = jnp.zeros((MOD_ROWS, d), F32).at[0].set(c[0]).at[1].set(c_ctx)
    mods = _adaln(cond, mod_w_down, mod_w_up, mod_b)
    mods = mods.reshape(depth, MOD_ROWS, N_MOD, d).swapaxes(1, 2)

    cos, sin = _rope_tables(n_lat, n_ctx)
    nbr_bias = _nbr_bias_tables(rpb, n_lat, n_ctx)
    qk_g = qk_norm_g.reshape(depth, 2, 1, HEAD_DIM)
    sgu_g = sgu_norm_g.reshape(depth, 1, W_C)

    xs = jnp.concatenate([x[0], ctx[0]], axis=0)

    for l in range(depth):
        h = _norm_mod(xs, norm_g4, mods, l, 0, n_lat)
        xs = _ffn_down(_ffn_up(h, ffn_w_in, l, 0), ffn_w_out_b, xs, mods, l, 0, 2, n_lat)

        h = _norm_mod(xs, norm_g4, mods, l, 1, n_lat)
        oa, ob, oc = _mixer_groups(h, l, mix_w_in_b, qk_g, sgu_g, cos, sin, nbr_bias, sgu_w_b, sgu_b_t,
                                   n_lat, n_ctx)
        xs = _out_proj(oa, ob, oc, mix_w_out_b, xs, mods, l, n_lat)

        h = _norm_mod(xs, norm_g4, mods, l, 2, n_lat)
        xs = _ffn_down(_ffn_up(h, ffn_w_in, l, 1), ffn_w_out_b, xs, mods, l, 1, 8, n_lat)

    return _final_norm(xs, final_norm_g, n_lat).reshape(1, n_lat, d)
```

```python
import functools
import math

import jax
import jax.numpy as jnp
import numpy as np
from jax import lax
from jax.experimental import pallas as pl
from jax.experimental.pallas import tpu as pltpu

HEAD_DIM = 128
GRID_W = 64
N_HEADS_A = 16
N_KV_A = 4
N_HEADS_B = 8
N_GROUPS_C = 8
WIN_H = 8
WIN_W = 16
CHUNK = 128
N_MOD = 9
ROPE_THETA = 10000.0
EPS = 1e-6
W_A_Q = N_HEADS_A * HEAD_DIM
W_A_KV = N_KV_A * HEAD_DIM
W_B = N_HEADS_B * HEAD_DIM
W_C = N_GROUPS_C * HEAD_DIM
GQA_REP = N_HEADS_A // N_KV_A

NBR_Q = 2 * GRID_W
NBR_KV_BLOCKS = 5
NBR_KEYS = NBR_KV_BLOCKS * NBR_Q
NBR_PATTERNS = 6
MASK_VALUE = -1e30
LOG2_E = 1.4426950408889634

V7X_VMEM_BYTES = 64 * 1024 * 1024
VMEM_LIMIT_BYTES = V7X_VMEM_BYTES - 8 * 1024 * 1024
V7X_MXU_COLS = 256
LANES = 128
GATTN_Q_ROWS = 256
GATTN_KV_CHUNK = 3584
GATTN_MAX_UNROLL = 8
MOD_ROWS = 8
BF16_SUBLANES = 16
NORM_ROWS = BF16_SUBLANES

TILE_FFN_UP = (2080, V7X_MXU_COLS)
TILE_FFN_DOWN = (1040, 512)
TILE_PROJ = (1040, 1024)
TILE_OUT_PROJ = (1664, 512)
TILE_NORM_ROWS = 256
TILE_SGU_ROWS = 1280
TILE_ADALN_COLS = 4096

F32 = jnp.float32
BF16 = jnp.bfloat16


def _tile(n, target, align):
    best = None
    for t in range(align, min(n, target) + 1, align):
        if n % t == 0:
            best = t
    assert best is not None, (n, target, align)
    return best


def _params(*sem):
    return pltpu.CompilerParams(dimension_semantics=sem, vmem_limit_bytes=VMEM_LIMIT_BYTES)


def _row_select(mod_ref, tm, n_lat):
    row = pl.program_id(0) * tm + lax.broadcasted_iota(jnp.int32, (tm, 1), 0)
    return jnp.where(row >= n_lat, mod_ref[1:2, :], mod_ref[0:1, :])


def _dot(a, b):
    return jnp.dot(a, b, preferred_element_type=F32)


def _adaln_kernel(cond_ref, wd_ref, wu_ref, b_ref, o_ref, mid_ref):
    @pl.when(pl.program_id(1) == 0)
    def _():
        s = cond_ref[...]
        s = s * (1.0 / (1.0 + jnp.exp(-s)))
        mid_ref[...] = jnp.dot(s, wd_ref[...], preferred_element_type=F32, precision=lax.Precision.HIGHEST)

    o_ref[...] = jnp.dot(mid_ref[...], wu_ref[...], preferred_element_type=F32,
                         precision=lax.Precision.HIGHEST) + b_ref[...]


def _adaln(cond, w_down, w_up, b_up):
    depth, d, r = w_down.shape
    n = w_up.shape[-1]
    tn = _tile(n, TILE_ADALN_COLS, LANES)
    return pl.pallas_call(
        _adaln_kernel,
        grid=(depth, n // tn),
        in_specs=[
            pl.BlockSpec((MOD_ROWS, d), lambda l, j: (0, 0)),
            pl.BlockSpec((None, d, r), lambda l, j: (l, 0, 0)),
            pl.BlockSpec((None, r, tn), lambda l, j: (l, 0, j)),
            pl.BlockSpec((None, 1, tn), lambda l, j: (l, 0, j)),
        ],
        out_specs=pl.BlockSpec((None, MOD_ROWS, tn), lambda l, j: (l, 0, j)),
        out_shape=jax.ShapeDtypeStruct((depth, MOD_ROWS, n), F32),
        scratch_shapes=[pltpu.VMEM((MOD_ROWS, r), F32)],
        compiler_params=_params("parallel", "arbitrary"),
        name="adaln",
    )(cond, w_down, w_up, b_up.reshape(depth, 1, n))


def _norm_mod_kernel(x_ref, g_ref, shift_ref, scale_ref, o_ref, *, tr, n_lat):
    which = (pl.program_id(0) * tr >= n_lat).astype(jnp.int32)
    gain = g_ref[...] * (1.0 + scale_ref[pl.ds(which, 1), :])
    shift = shift_ref[pl.ds(which, 1), :]

    def body(r, carry):
        rs = pl.ds(pl.multiple_of(r * NORM_ROWS, NORM_ROWS), NORM_ROWS)
        x = x_ref[rs, :]
        inv = lax.rsqrt(jnp.mean(x * x, axis=-1, keepdims=True) + EPS)
        o_ref[rs, :] = (x_ref[rs, :] * inv * gain + shift).astype(o_ref.dtype)
        return carry

    lax.fori_loop(0, tr // NORM_ROWS, body, 0, unroll=2)


def _norm_mod(x, norm_g, mods, layer, sub, n_lat):
    n, d = x.shape
    tr = _tile(math.gcd(n_lat, n - n_lat), TILE_NORM_ROWS, NORM_ROWS)
    kern = functools.partial(_norm_mod_kernel, tr=tr, n_lat=n_lat)
    return pl.pallas_call(
        kern,
        grid=(n // tr,),
        in_specs=[
            pl.BlockSpec((tr, d), lambda i: (i, 0)),
            pl.BlockSpec((None, None, 1, d), lambda i: (layer, sub, 0, 0)),
            pl.BlockSpec((None, None, MOD_ROWS, d), lambda i: (layer, 3 * sub, 0, 0)),
            pl.BlockSpec((None, None, MOD_ROWS, d), lambda i: (layer, 3 * sub + 1, 0, 0)),
        ],
        out_specs=pl.BlockSpec((tr, d), lambda i: (i, 0)),
        out_shape=jax.ShapeDtypeStruct((n, d), BF16),
        compiler_params=_params("parallel"),
        name="norm_mod",
    )(x, norm_g, mods, mods)


def _final_norm_kernel(x_ref, g_ref, o_ref):
    g = g_ref[...]

    def body(r, carry):
        rs = pl.ds(pl.multiple_of(r * NORM_ROWS, NORM_ROWS), NORM_ROWS)
        x = x_ref[rs, :]
        inv = lax.rsqrt(jnp.mean(x * x, axis=-1, keepdims=True) + EPS)
        o_ref[rs, :] = x_ref[rs, :] * inv * g
        return carry

    lax.fori_loop(0, x_ref.shape[0] // NORM_ROWS, body, 0, unroll=2)


def _final_norm(x, g, n_lat):
    d = x.shape[1]
    tr = _tile(n_lat, TILE_NORM_ROWS, NORM_ROWS)
    return pl.pallas_call(
        _final_norm_kernel,
        grid=(n_lat // tr,),
        in_specs=[pl.BlockSpec((tr, d), lambda i: (i, 0)),
                  pl.BlockSpec((1, d), lambda i: (0, 0))],
        out_specs=pl.BlockSpec((tr, d), lambda i: (i, 0)),
        out_shape=jax.ShapeDtypeStruct((n_lat, d), F32),
        compiler_params=_params("parallel"),
        name="final_norm",
    )(x, g.reshape(1, d))


def _ffn_up_kernel(h_ref, wa_ref, wb_ref, o_ref):
    h = h_ref[...]
    a = _dot(h, wa_ref[...].astype(BF16))
    b = _dot(h, wb_ref[...].astype(BF16))
    o_ref[...] = (a * (1.0 / (1.0 + jnp.exp(-a))) * b).astype(o_ref.dtype)


def _ffn_up(h, w_in, layer, half):
    n, d = h.shape
    f = w_in.shape[-1] // 2
    tm = _tile(n, TILE_FFN_UP[0], BF16_SUBLANES)
    tn = _tile(f, TILE_FFN_UP[1], LANES)
    nb = f // tn
    return pl.pallas_call(
        _ffn_up_kernel,
        grid=(n // tm, nb),
        in_specs=[
            pl.BlockSpec((tm, d), lambda i, j: (i, 0)),
            pl.BlockSpec((None, None, d, tn), lambda i, j: (layer, half, 0, j)),
            pl.BlockSpec((None, None, d, tn), lambda i, j: (layer, half, 0, j + nb)),
        ],
        out_specs=pl.BlockSpec((tm, tn), lambda i, j: (i, j)),
        out_shape=jax.ShapeDtypeStruct((n, f), BF16),
        compiler_params=_params("parallel", "parallel"),
        name="ffn_up",
    )(h, w_in, w_in)


def _ffn_down_kernel(g_ref, w_ref, x_ref, gate_ref, o_ref, *, tm, n_lat):
    y = _dot(g_ref[...], w_ref[...])
    o_ref[...] = x_ref[...] + 0.5 * _row_select(gate_ref, tm, n_lat) * y


def _ffn_down(g, w_out, x, mods, layer, half, gate_idx, n_lat):
    n, f = g.shape
    d = x.shape[1]
    tm = _tile(n, TILE_FFN_DOWN[0], BF16_SUBLANES)
    tn = _tile(d, TILE_FFN_DOWN[1], LANES)
    kern = functools.partial(_ffn_down_kernel, tm=tm, n_lat=n_lat)
    return pl.pallas_call(
        kern,
        grid=(n // tm, d // tn),
        in_specs=[
            pl.BlockSpec((tm, f), lambda i, j: (i, 0)),
            pl.BlockSpec((None, None, f, tn), lambda i, j: (layer, half, 0, j)),
            pl.BlockSpec((tm, tn), lambda i, j: (i, j)),
            pl.BlockSpec((None, None, MOD_ROWS, tn), lambda i, j: (layer, gate_idx, 0, j)),
        ],
        out_specs=pl.BlockSpec((tm, tn), lambda i, j: (i, j)),
        out_shape=jax.ShapeDtypeStruct((n, d), F32),
        compiler_params=_params("parallel", "parallel"),
        name="ffn_down",
    )(g, w_out, x, mods)


def _head_rms(x, g):
    return x * lax.rsqrt(jnp.mean(x * x, axis=-1, keepdims=True) + EPS) * g


def _rope(x, cos, sin_signed):
    lane = lax.broadcasted_iota(jnp.int32, x.shape, 1)
    low = jnp.bitwise_and(lane, HEAD_DIM // 2 - 1) < (HEAD_DIM // 4)
    rot = jnp.where(low, pltpu.roll(x, HEAD_DIM - HEAD_DIM // 4, 1), pltpu.roll(x, HEAD_DIM // 4, 1))
    return x * cos + rot * sin_signed


def _epi_qk_rope(acc, cols, g_ref, cos_ref, sin_ref, *, out_scale):
    g = g_ref[...]
    cos = cos_ref[...]
    sin = sin_ref[...]
    heads = []
    for h in range(acc.shape[1] // HEAD_DIM):
        x = _head_rms(acc[:, h * HEAD_DIM:(h + 1) * HEAD_DIM], g)
        heads.append(_rope(x, cos, sin) * out_scale)
    return jnp.concatenate(heads, axis=1)


def _epi_plain(acc, cols):
    return acc


def _epi_qkv_b(acc, cols):
    return acc * jnp.where(pl.program_id(1) == 0, HEAD_DIM ** -0.5 * LOG2_E, 1.0)


def _epi_kv_a(acc, cols, g_ref, cos_ref, sin_ref):
    if cols.start < W_A_KV:
        return _epi_qk_rope(acc, cols, g_ref, cos_ref, sin_ref, out_scale=1.0)
    return acc


def _gelu_tanh(x):
    return 0.5 * x * (1.0 + jnp.tanh(0.7978845608028654 * (x + 0.044715 * (x * x * x))))


def _epi_gelu(acc, cols):
    return _gelu_tanh(acc)


def _epi_gelu_rms(acc, cols, g_ref):
    y = _gelu_tanh(acc)
    g = g_ref[:, cols]
    groups = []
    for c in range(acc.shape[1] // HEAD_DIM):
        sl = slice(c * HEAD_DIM, (c + 1) * HEAD_DIM)
        groups.append(_head_rms(y[:, sl], g[:, sl]))
    return jnp.concatenate(groups, axis=1)


def _proj_kernel(h_ref, w_ref, *rest, epilogue, chunk):
    extras, o_ref = rest[:-1], rest[-1]
    h = h_ref[...]
    for c in range(o_ref.shape[1] // chunk):
        cols = slice(c * chunk, (c + 1) * chunk)
        o_ref[:, cols] = epilogue(_dot(h, w_ref[:, cols]), cols, *extras).astype(o_ref.dtype)


def _proj(h, w_in, layer, col_off, n_cols, epilogue, extras, extra_specs, out_dtype, name, chunk=None, tn=None):
    n, d = h.shape
    tm = _tile(n, TILE_PROJ[0], BF16_SUBLANES)
    tn = _tile(n_cols, TILE_PROJ[1], LANES) if tn is None else tn
    chunk = tn if chunk is None else chunk
    assert col_off % tn == 0 and tn % chunk == 0
    off = col_off // tn
    return pl.pallas_call(
        functools.partial(_proj_kernel, epilogue=epilogue, chunk=chunk),
        grid=(n // tm, n_cols // tn),
        in_specs=[
            pl.BlockSpec((tm, d), lambda i, j: (i, 0)),
            pl.BlockSpec((None, d, tn), lambda i, j: (layer, 0, j + off)),
        ] + [spec(tm, tn) for spec in extra_specs],
        out_specs=pl.BlockSpec((tm, tn), lambda i, j: (i, j)),
        out_shape=jax.ShapeDtypeStruct((n, n_cols), out_dtype),
        compiler_params=_params("parallel", "parallel"),
        name=name,
    )(h, w_in, *extras)


def _gattn_kernel(q_ref, kt_ref, v_ref, o_ref, qs_ref, m_ref, l_ref, acc_ref, *, tq, tk, n_lat, n_ctx):
    for h in range(GQA_REP):
        qs_ref[h * tq:(h + 1) * tq, :] = q_ref[:, h * HEAD_DIM:(h + 1) * HEAD_DIM]
    m_ref[...] = jnp.full(m_ref.shape, MASK_VALUE, F32)
    l_ref[...] = jnp.zeros(l_ref.shape, F32)
    acc_ref[...] = jnp.zeros(acc_ref.shape, F32)

    def step(kt, v):
        ncol = v.shape[0] // HEAD_DIM
        s = _dot(qs_ref[...], kt)
        m_old = m_ref[...]
        m_new = jnp.maximum(m_old, jnp.max(s, axis=1, keepdims=True))
        alpha = jnp.exp2(m_old - m_new)
        p = jnp.exp2(s - jnp.concatenate([m_new] * ncol, axis=1))
        psum = p[:, :HEAD_DIM]
        for j in range(1, ncol):
            psum = psum + p[:, j * HEAD_DIM:(j + 1) * HEAD_DIM]
        l_ref[...] = alpha * l_ref[...] + psum
        acc_ref[...] = alpha * acc_ref[...] + _dot(p.astype(BF16), v)
        m_ref[...] = m_new

    def body(c, carry):
        off = pl.multiple_of(c * tk, tk)
        step(kt_ref[:, pl.ds(off, tk)], v_ref[pl.ds(off, tk), :])
        return carry

    n_chunks = (n_lat + n_ctx) // tk
    is_ctx = pl.program_id(1) * tq >= n_lat

    @pl.when(jnp.logical_not(is_ctx))
    def _():
        lax.fori_loop(0, n_chunks, body, 0, unroll=n_chunks if n_chunks <= GATTN_MAX_UNROLL else 1)

    @pl.when(is_ctx)
    def _():
        step(kt_ref[:, n_lat:n_lat + n_ctx], v_ref[n_lat:n_lat + n_ctx, :])

    o = acc_ref[...] / jnp.sum(l_ref[...], axis=1, keepdims=True)
    for h in range(GQA_REP):
        o_ref[:, h * HEAD_DIM:(h + 1) * HEAD_DIM] = o[h * tq:(h + 1) * tq].astype(o_ref.dtype)


def _global_attn(qa, ka_t, kv_a, n_lat, n_ctx):
    n = qa.shape[0]
    tq = _tile(n_ctx, GATTN_Q_ROWS, BF16_SUBLANES)
    assert n_lat % tq == 0 and n_ctx % tq == 0
    tk = _tile(n, GATTN_KV_CHUNK, V7X_MXU_COLS)
    gw = GQA_REP * HEAD_DIM
    kern = functools.partial(_gattn_kernel, tq=tq, tk=tk, n_lat=n_lat, n_ctx=n_ctx)
    return pl.pallas_call(
        kern,
        grid=(N_KV_A, n // tq),
        in_specs=[
            pl.BlockSpec((tq, gw), lambda g, i: (i, g)),
            pl.BlockSpec((HEAD_DIM, n), lambda g, i: (g, 0)),
            pl.BlockSpec((n, HEAD_DIM), lambda g, i: (0, N_KV_A + g)),
        ],
        out_specs=pl.BlockSpec((tq, gw), lambda g, i: (i, g)),
        out_shape=jax.ShapeDtypeStruct((n, W_A_Q), BF16),
        scratch_shapes=[
            pltpu.VMEM((GQA_REP * tq, HEAD_DIM), BF16),
            pltpu.VMEM((GQA_REP * tq, HEAD_DIM), F32),
            pltpu.VMEM((GQA_REP * tq, HEAD_DIM), F32),
            pltpu.VMEM((GQA_REP * tq, HEAD_DIM), F32),
        ],
        compiler_params=_params("parallel", "parallel"),
        name="global_attn",
    )(qa, ka_t, kv_a)


def _nbr_kernel(q_ref, *rest):
    k_refs = rest[:NBR_KV_BLOCKS]
    v_refs = rest[NBR_KV_BLOCKS:2 * NBR_KV_BLOCKS]
    kc_ref, vc_ref, bias_ref, o_ref = rest[2 * NBR_KV_BLOCKS:]
    nt = (((1,), (1,)), ((), ()))
    heads = [slice(h * HEAD_DIM, (h + 1) * HEAD_DIM) for h in range(N_HEADS_B)]
    scores = []
    for h, sl in enumerate(heads):
        k = jnp.concatenate([r[:, sl] for r in k_refs] + [kc_ref[:, sl]], axis=0)
        scores.append(lax.dot_general(q_ref[:, sl], k, nt, preferred_element_type=F32) + bias_ref[h])
    probs, sums = [], []
    for s in scores:
        p = jnp.exp2(s - jnp.max(s, axis=1, keepdims=True))
        sums.append(jnp.sum(p, axis=1, keepdims=True))
        probs.append(p.astype(BF16))
    for h, sl in enumerate(heads):
        v = jnp.concatenate([r[:, sl] for r in v_refs] + [vc_ref[:, sl]], axis=0)
        o_ref[:, sl] = (_dot(probs[h], v) / sums[h]).astype(o_ref.dtype)


def _nbr_bias_tables(rpb, n_lat, n_ctx):
    rows = n_lat // GRID_W
    nblk = n_lat // NBR_Q
    kh = min(WIN_H, rows)
    q_rows, kv_rows = NBR_Q // GRID_W, NBR_KEYS // GRID_W
    n_off_r, n_off_c = 2 * WIN_H - 1, 2 * WIN_W - 1
    blocks = (0, 1, 2, nblk - 2, nblk - 1)
    sel_r = np.zeros((len(blocks), n_off_r, q_rows, kv_rows), np.float32)
    for p, blk in enumerate(blocks):
        blk0 = min(max(blk - 2, 0), nblk - NBR_KV_BLOCKS)
        for qr in range(q_rows):
            rq = q_rows * blk + qr
            r0 = min(max(rq - kh // 2, 0), rows - kh)
            for kr in range(kv_rows):
                key_row = q_rows * blk0 + kr
                if r0 <= key_row < r0 + kh:
                    sel_r[p, key_row - rq + WIN_H - 1, qr, kr] = 1.0
    sel_c = np.zeros((n_off_c, GRID_W, GRID_W), np.float32)
    for qc in range(GRID_W):
        c0 = min(max(qc - WIN_W // 2, 0), GRID_W - WIN_W)
        for kc in range(c0, c0 + WIN_W):
            sel_c[kc - qc + WIN_W - 1, qc, kc] = 1.0
    inside = np.einsum("paqk,bcd->pqckd", sel_r, sel_c).reshape(len(blocks), NBR_Q, NBR_KEYS) > 0.5
    bias = jnp.einsum("lhab,paqk,bcd->lphqckd", rpb.astype(F32), sel_r, sel_c,
                      precision=lax.Precision.HIGHEST)
    bias = bias.reshape(rpb.shape[0], len(blocks), N_HEADS_B, NBR_Q, NBR_KEYS) * LOG2_E
    bias = jnp.where(inside[None, :, None], bias, MASK_VALUE)
    bias = jnp.concatenate([bias, jnp.full_like(bias[:, :1], MASK_VALUE)], axis=1)
    return jnp.pad(bias, ((0, 0),) * 4 + ((0, n_ctx),))


def _nbr_attn(qkv_b, bias, layer, n_lat, n_ctx):
    n = qkv_b.shape[0]
    nblk = n_lat // NBR_Q
    assert nblk >= NBR_KV_BLOCKS + 1 and n_ctx % NBR_Q == 0
    ctx_blk = n_lat // n_ctx
    assert n_lat % n_ctx == 0

    def first_kv(i):
        return jnp.clip(i - 2, 0, nblk - NBR_KV_BLOCKS)

    def pattern(i):
        return jnp.where(i >= nblk, NBR_PATTERNS - 1, i - first_kv(i))

    def kv_spec(col, b):
        return pl.BlockSpec((NBR_Q, W_B), lambda i: (first_kv(i) + b, col))

    in_specs = ([pl.BlockSpec((NBR_Q, W_B), lambda i: (i, 0))]
                + [kv_spec(1, b) for b in range(NBR_KV_BLOCKS)]
                + [kv_spec(2, b) for b in range(NBR_KV_BLOCKS)]
                + [pl.BlockSpec((n_ctx, W_B), lambda i: (ctx_blk, 1)),
                   pl.BlockSpec((n_ctx, W_B), lambda i: (ctx_blk, 2)),
                   pl.BlockSpec((None, None, N_HEADS_B, NBR_Q, NBR_KEYS + n_ctx),
                                lambda i: (layer, pattern(i), 0, 0, 0))])
    return pl.pallas_call(
        _nbr_kernel,
        grid=(n // NBR_Q,),
        in_specs=in_specs,
        out_specs=pl.BlockSpec((NBR_Q, W_B), lambda i: (i, 0)),
        out_shape=jax.ShapeDtypeStruct((n, W_B), BF16),
        compiler_params=_params("parallel"),
        name="nbr_attn",
    )(*([qkv_b] * (3 + 2 * NBR_KV_BLOCKS)), bias)


def _sgu_kernel(u_ref, vn_ref, w_ref, bt_ref, o_ref, *, n_chunks):
    for g in range(N_GROUPS_C):
        sl = slice(g * HEAD_DIM, (g + 1) * HEAD_DIM)
        w = w_ref[g]
        b = bt_ref[:, g:g + 1]
        for c in range(n_chunks):
            rs = slice(c * CHUNK, (c + 1) * CHUNK)
            sv = _dot(w, vn_ref[rs, sl]) + b
            o_ref[rs, sl] = (u_ref[rs, sl] * sv).astype(o_ref.dtype)


def _sgu(u, vn, w_s, b_t, layer):
    n = u.shape[0]
    tr = _tile(n, TILE_SGU_ROWS, CHUNK)
    kern = functools.partial(_sgu_kernel, n_chunks=tr // CHUNK)
    return pl.pallas_call(
        kern,
        grid=(n // tr,),
        in_specs=[
            pl.BlockSpec((tr, W_C), lambda i: (i, 0)),
            pl.BlockSpec((tr, W_C), lambda i: (i, 0)),
            pl.BlockSpec((None, N_GROUPS_C, CHUNK, CHUNK), lambda i: (layer, 0, 0, 0)),
            pl.BlockSpec((None, CHUNK, N_GROUPS_C), lambda i: (layer, 0, 0)),
        ],
        out_specs=pl.BlockSpec((tr, W_C), lambda i: (i, 0)),
        out_shape=jax.ShapeDtypeStruct((n, W_C), BF16),
        compiler_params=_params("parallel"),
        name="sgu",
    )(u, vn, w_s, b_t)


def _out_proj_kernel(oa_ref, ob_ref, oc_ref, wa_ref, wb_ref, wc_ref, x_ref, gate_ref, o_ref, *, tm, n_lat):
    y = _dot(oa_ref[...], wa_ref[...]) + _dot(ob_ref[...], wb_ref[...]) + _dot(oc_ref[...], wc_ref[...])
    o_ref[...] = x_ref[...] + _row_select(gate_ref, tm, n_lat) * y


def _out_proj(oa, ob, oc, w_out, x, mods, layer, n_lat):
    n, d = x.shape
    tm = _tile(n, TILE_OUT_PROJ[0], BF16_SUBLANES)
    tn = _tile(d, TILE_OUT_PROJ[1], LANES)
    assert W_A_Q % W_B == 0 and W_B == W_C
    kern = functools.partial(_out_proj_kernel, tm=tm, n_lat=n_lat)
    return pl.pallas_call(
        kern,
        grid=(n // tm, d // tn),
        in_specs=[
            pl.BlockSpec((tm, W_A_Q), lambda i, j: (i, 0)),
            pl.BlockSpec((tm, W_B), lambda i, j: (i, 0)),
            pl.BlockSpec((tm, W_C), lambda i, j: (i, 0)),
            pl.BlockSpec((None, W_A_Q, tn), lambda i, j: (layer, 0, j)),
            pl.BlockSpec((None, W_B, tn), lambda i, j: (layer, W_A_Q // W_B, j)),
            pl.BlockSpec((None, W_C, tn), lambda i, j: (layer, W_A_Q // W_B + 1, j)),
            pl.BlockSpec((tm, tn), lambda i, j: (i, j)),
            pl.BlockSpec((None, None, MOD_ROWS, tn), lambda i, j: (layer, 5, 0, j)),
        ],
        out_specs=pl.BlockSpec((tm, tn), lambda i, j: (i, j)),
        out_shape=jax.ShapeDtypeStruct((n, d), F32),
        compiler_params=_params("parallel", "parallel"),
        name="out_proj",
    )(oa, ob, oc, w_out, w_out, w_out, x, mods)


def _rope_tables(n_lat, n_ctx):
    t = jnp.arange(n_lat, dtype=jnp.int32)
    row = (t // GRID_W).astype(F32)
    col = (t % GRID_W).astype(F32)
    half = HEAD_DIM // 2
    inv = ROPE_THETA ** (-jnp.arange(0, half, 2, dtype=F32) / half)
    ang_r = row[:, None] * inv[None, :]
    ang_c = col[:, None] * inv[None, :]
    ang = jnp.concatenate([ang_r, ang_r, ang_c, ang_c], axis=-1)
    lane = jnp.arange(HEAD_DIM)
    sign = jnp.where((lane % half) < HEAD_DIM // 4, -1.0, 1.0).astype(F32)
    cos = jnp.concatenate([jnp.cos(ang), jnp.ones((n_ctx, HEAD_DIM), F32)], axis=0)
    sin = jnp.concatenate([jnp.sin(ang) * sign, jnp.zeros((n_ctx, HEAD_DIM), F32)], axis=0)
    return cos, sin


def _mixer_groups(h, l, mix_w_in_b, qk_g, sgu_g, cos, sin, nbr_bias, sgu_w_b, sgu_b_t, n_lat, n_ctx):
    rope_specs = [lambda tm, tn: pl.BlockSpec((tm, HEAD_DIM), lambda i, j: (i, 0))] * 2

    def g_spec(which):
        return lambda tm, tn: pl.BlockSpec((None, None, 1, HEAD_DIM), lambda i, j: (l, which, 0, 0))

    col_ka = W_A_Q
    col_b = col_ka + 2 * W_A_KV
    col_u = col_b + 3 * W_B
    col_v = col_u + W_C
    qa = _proj(h, mix_w_in_b, l, 0, W_A_Q,
               functools.partial(_epi_qk_rope, out_scale=HEAD_DIM ** -0.5 * LOG2_E),
               [qk_g, cos, sin], [g_spec(0)] + rope_specs, BF16, "proj_qa", chunk=V7X_MXU_COLS)
    kv_a = _proj(h, mix_w_in_b, l, col_ka, 2 * W_A_KV, _epi_kv_a,
                 [qk_g, cos, sin], [g_spec(1)] + rope_specs, BF16, "proj_kv_a", chunk=V7X_MXU_COLS)
    qkv_b = _proj(h, mix_w_in_b, l, col_b, 3 * W_B, _epi_qkv_b, [], [], BF16, "proj_b", tn=W_B)
    u = _proj(h, mix_w_in_b, l, col_u, W_C, _epi_gelu, [], [], F32, "proj_u", chunk=V7X_MXU_COLS)
    vn = _proj(h, mix_w_in_b, l, col_v, W_C, _epi_gelu_rms, [sgu_g],
               [lambda tm, tn: pl.BlockSpec((None, 1, tn), lambda i, j: (l, 0, j))], BF16, "proj_v",
               chunk=V7X_MXU_COLS)
    oa = _global_attn(qa, kv_a[:, :W_A_KV].T, kv_a, n_lat, n_ctx)
    ob = _nbr_attn(qkv_b, nbr_bias, l, n_lat, n_ctx)
    oc = _sgu(u, vn, sgu_w_b, sgu_b_t, l)
    return oa, ob, oc


def kernel(x, c, ctx, c_ctx, norm_g, mod_w_down, mod_w_up, mod_b, ffn_w_in, ffn_w_out, mix_w_in, mix_w_out,
           qk_norm_g, rpb, sgu_norm_g, sgu_w, sgu_b, final_norm_g):
    batch, n_lat, d = x.shape
    n_ctx = ctx.shape[1]
    depth = norm_g.shape[0]
    assert batch == 1 and c.shape[0] == 1

    ffn_w_out_b = ffn_w_out.astype(BF16)
    mix_w_in_b = mix_w_in.astype(BF16)
    mix_w_out_b = mix_w_out.astype(BF16)
    sgu_w_b = sgu_w.astype(BF16)
    sgu_b_t = jnp.swapaxes(sgu_b, 1, 2)
    norm_g4 = norm_g.reshape(depth, 3, 1, d)

    cond = jnp.zeros((MOD_ROWS, d), F32).at[0].set(c[0]).at[1].set(c_ctx)
    mods = _adaln(cond, mod_w_down, mod_w_up, mod_b)
    mods = mods.reshape(depth, MOD_ROWS, N_MOD, d).swapaxes(1, 2)

    cos, sin = _rope_tables(n_lat, n_ctx)
    nbr_bias = _nbr_bias_tables(rpb, n_lat, n_ctx)
    qk_g = qk_norm_g.reshape(depth, 2, 1, HEAD_DIM)
    sgu_g = sgu_norm_g.reshape(depth, 1, W_C)

    xs = jnp.concatenate([x[0], ctx[0]], axis=0)

    for l in range(depth):
        h = _norm_mod(xs, norm_g4, mods, l, 0, n_lat)
        xs = _ffn_down(_ffn_up(h, ffn_w_in, l, 0), ffn_w_out_b, xs, mods, l, 0, 2, n_lat)

        h = _norm_mod(xs, norm_g4, mods, l, 1, n_lat)
        oa, ob, oc = _mixer_groups(h, l, mix_w_in_b, qk_g, sgu_g, cos, sin, nbr_bias, sgu_w_b, sgu_b_t,
                                   n_lat, n_ctx)
        xs = _out_proj(oa, ob, oc, mix_w_out_b, xs, mods, l, n_lat)

        h = _norm_mod(xs, norm_g4, mods, l, 2, n_lat)
        xs = _ffn_down(_ffn_up(h, ffn_w_in, l, 1), ffn_w_out_b, xs, mods, l, 1, 8, n_lat)

    return _final_norm(xs, final_norm_g, n_lat).reshape(1, n_lat, d)
```

```python
import functools
import math

import jax
import jax.numpy as jnp
import numpy as np
from jax import lax
from jax.experimental import pallas as pl
from jax.experimental.pallas import tpu as pltpu

HEAD_DIM = 128
GRID_W = 64
N_HEADS_A = 16
N_KV_A = 4
N_HEADS_B = 8
N_GROUPS_C = 8
WIN_H = 8
WIN_W = 16
CHUNK = 128
N_MOD = 9
ROPE_THETA = 10000.0
EPS = 1e-6
W_A_Q = N_HEADS_A * HEAD_DIM
W_A_KV = N_KV_A * HEAD_DIM
W_B = N_HEADS_B * HEAD_DIM
W_C = N_GROUPS_C * HEAD_DIM
GQA_REP = N_HEADS_A // N_KV_A

NBR_Q = 2 * GRID_W
NBR_KV_BLOCKS = 5
NBR_KEYS = NBR_KV_BLOCKS * NBR_Q
NBR_PATTERNS = 6
MASK_VALUE = -1e30
LOG2_E = 1.4426950408889634

V7X_VMEM_BYTES = 64 * 1024 * 1024
VMEM_LIMIT_BYTES = V7X_VMEM_BYTES - 8 * 1024 * 1024
V7X_MXU_COLS = 256
LANES = 128
GATTN_Q_ROWS = 256
GATTN_KV_CHUNK = 3584
GATTN_MAX_UNROLL = 8
MOD_ROWS = 8
BF16_SUBLANES = 16
NORM_ROWS = BF16_SUBLANES
NORM_IN_BUFFERS = 3

TILE_FFN_UP = (2080, V7X_MXU_COLS)
TILE_FFN_DOWN = (1040, 512)
TILE_PROJ = (1040, 1024)
TILE_OUT_PROJ = (1664, 512)
TILE_NORM_ROWS = 256
TILE_SGU_ROWS = 1280
TILE_ADALN_COLS = 4096

F32 = jnp.float32
BF16 = jnp.bfloat16


def _tile(n, target, align):
    best = None
    for t in range(align, min(n, target) + 1, align):
        if n % t == 0:
            best = t
    assert best is not None, (n, target, align)
    return best


def _params(*sem):
    return pltpu.CompilerParams(dimension_semantics=sem, vmem_limit_bytes=VMEM_LIMIT_BYTES)


def _row_select(mod_ref, tm, n_lat):
    row = pl.program_id(0) * tm + lax.broadcasted_iota(jnp.int32, (tm, 1), 0)
    return jnp.where(row >= n_lat, mod_ref[1:2, :], mod_ref[0:1, :])


def _dot(a, b):
    return jnp.dot(a, b, preferred_element_type=F32)


def _adaln_kernel(cond_ref, wd_ref, wu_ref, b_ref, o_ref, mid_ref):
    @pl.when(pl.program_id(1) == 0)
    def _():
        s = cond_ref[...]
        s = s * (1.0 / (1.0 + jnp.exp(-s)))
        mid_ref[...] = jnp.dot(s, wd_ref[...], preferred_element_type=F32, precision=lax.Precision.HIGHEST)

    o_ref[...] = jnp.dot(mid_ref[...], wu_ref[...], preferred_element_type=F32,
                         precision=lax.Precision.HIGHEST) + b_ref[...]


def _adaln(cond, w_down, w_up, b_up):
    depth, d, r = w_down.shape
    n = w_up.shape[-1]
    tn = _tile(n, TILE_ADALN_COLS, LANES)
    return pl.pallas_call(
        _adaln_kernel,
        grid=(depth, n // tn),
        in_specs=[
            pl.BlockSpec((MOD_ROWS, d), lambda l, j: (0, 0)),
            pl.BlockSpec((None, d, r), lambda l, j: (l, 0, 0)),
            pl.BlockSpec((None, r, tn), lambda l, j: (l, 0, j)),
            pl.BlockSpec((None, 1, tn), lambda l, j: (l, 0, j)),
        ],
        out_specs=pl.BlockSpec((None, MOD_ROWS, tn), lambda l, j: (l, 0, j)),
        out_shape=jax.ShapeDtypeStruct((depth, MOD_ROWS, n), F32),
        scratch_shapes=[pltpu.VMEM((MOD_ROWS, r), F32)],
        compiler_params=_params("parallel", "arbitrary"),
        name="adaln",
    )(cond, w_down, w_up, b_up.reshape(depth, 1, n))


def _norm_mod_kernel(x_hbm, g_ref, shift_ref, scale_ref, o_hbm, *, tr, n_lat, n_ctx):
    d = g_ref.shape[-1]

    def run(which, start, rows):
        gain = g_ref[...] * (1.0 + scale_ref[which:which + 1, :])
        shift = shift_ref[which:which + 1, :]

        def tile(x_ref, o_ref):
            def body(r, carry):
                rs = pl.ds(pl.multiple_of(r * NORM_ROWS, NORM_ROWS), NORM_ROWS)
                x = x_ref[rs, :]
                inv = lax.rsqrt(jnp.mean(x * x, axis=-1, keepdims=True) + EPS)
                o_ref[rs, :] = (x_ref[rs, :] * inv * gain + shift).astype(o_ref.dtype)
                return carry

            lax.fori_loop(0, tr // NORM_ROWS, body, 0, unroll=2)

        pltpu.emit_pipeline(
            tile, grid=(rows // tr,),
            in_specs=[pl.BlockSpec((tr, d), lambda i: (i, 0), pipeline_mode=pl.Buffered(NORM_IN_BUFFERS))],
            out_specs=[pl.BlockSpec((tr, d), lambda i: (i, 0))],
        )(x_hbm.at[pl.ds(start, rows)], o_hbm.at[pl.ds(start, rows)])

    run(0, 0, n_lat)
    run(1, n_lat, n_ctx)


def _norm_mod(x, norm_g, mods, layer, sub, n_lat):
    n, d = x.shape
    tr = _tile(math.gcd(n_lat, n - n_lat), TILE_NORM_ROWS, NORM_ROWS)
    kern = functools.partial(_norm_mod_kernel, tr=tr, n_lat=n_lat, n_ctx=n - n_lat)
    whole = pl.BlockSpec(memory_space=pltpu.VMEM)
    return pl.pallas_call(
        kern,
        in_specs=[pl.BlockSpec(memory_space=pl.ANY), whole, whole, whole],
        out_specs=pl.BlockSpec(memory_space=pl.ANY),
        out_shape=jax.ShapeDtypeStruct((n, d), BF16),
        compiler_params=pltpu.CompilerParams(vmem_limit_bytes=VMEM_LIMIT_BYTES),
        name="norm_mod",
    )(x, norm_g[layer, sub], mods[layer, 3 * sub], mods[layer, 3 * sub + 1])


def _final_norm_kernel(x_ref, g_ref, o_ref):
    g = g_ref[...]

    def body(r, carry):
        rs = pl.ds(pl.multiple_of(r * NORM_ROWS, NORM_ROWS), NORM_ROWS)
        x = x_ref[rs, :]
        inv = lax.rsqrt(jnp.mean(x * x, axis=-1, keepdims=True) + EPS)
        o_ref[rs, :] = x_ref[rs, :] * inv * g
        return carry

    lax.fori_loop(0, x_ref.shape[0] // NORM_ROWS, body, 0, unroll=2)


def _final_norm(x, g, n_lat):
    d = x.shape[1]
    tr = _tile(n_lat, TILE_NORM_ROWS, NORM_ROWS)
    return pl.pallas_call(
        _final_norm_kernel,
        grid=(n_lat // tr,),
        in_specs=[pl.BlockSpec((tr, d), lambda i: (i, 0)),
                  pl.BlockSpec((1, d), lambda i: (0, 0))],
        out_specs=pl.BlockSpec((tr, d), lambda i: (i, 0)),
        out_shape=jax.ShapeDtypeStruct((n_lat, d), F32),
        compiler_params=_params("parallel"),
        name="final_norm",
    )(x, g.reshape(1, d))


def _ffn_up_kernel(h_ref, wa_ref, wb_ref, o_ref):
    h = h_ref[...]
    a = _dot(h, wa_ref[...].astype(BF16))
    b = _dot(h, wb_ref[...].astype(BF16))
    o_ref[...] = (a * (1.0 / (1.0 + jnp.exp(-a))) * b).astype(o_ref.dtype)


def _ffn_up(h, w_in, layer, half):
    n, d = h.shape
    f = w_in.shape[-1] // 2
    tm = _tile(n, TILE_FFN_UP[0], BF16_SUBLANES)
    tn = _tile(f, TILE_FFN_UP[1], LANES)
    nb = f // tn
    return pl.pallas_call(
        _ffn_up_kernel,
        grid=(n // tm, nb),
        in_specs=[
            pl.BlockSpec((tm, d), lambda i, j: (i, 0)),
            pl.BlockSpec((None, None, d, tn), lambda i, j: (layer, half, 0, j)),
            pl.BlockSpec((None, None, d, tn), lambda i, j: (layer, half, 0, j + nb)),
        ],
        out_specs=pl.BlockSpec((tm, tn), lambda i, j: (i, j)),
        out_shape=jax.ShapeDtypeStruct((n, f), BF16),
        compiler_params=_params("parallel", "parallel"),
        name="ffn_up",
    )(h, w_in, w_in)


def _ffn_down_kernel(g_ref, w_ref, x_ref, gate_ref, o_ref, *, tm, n_lat):
    y = _dot(g_ref[...], w_ref[...])
    o_ref[...] = x_ref[...] + 0.5 * _row_select(gate_ref, tm, n_lat) * y


def _ffn_down(g, w_out, x, mods, layer, half, gate_idx, n_lat):
    n, f = g.shape
    d = x.shape[1]
    tm = _tile(n, TILE_FFN_DOWN[0], BF16_SUBLANES)
    tn = _tile(d, TILE_FFN_DOWN[1], LANES)
    kern = functools.partial(_ffn_down_kernel, tm=tm, n_lat=n_lat)
    return pl.pallas_call(
        kern,
        grid=(n // tm, d // tn),
        in_specs=[
            pl.BlockSpec((tm, f), lambda i, j: (i, 0)),
            pl.BlockSpec((None, None, f, tn), lambda i, j: (layer, half, 0, j)),
            pl.BlockSpec((tm, tn), lambda i, j: (i, j)),
            pl.BlockSpec((None, None, MOD_ROWS, tn), lambda i, j: (layer, gate_idx, 0, j)),
        ],
        out_specs=pl.BlockSpec((tm, tn), lambda i, j: (i, j)),
        out_shape=jax.ShapeDtypeStruct((n, d), F32),
        compiler_params=_params("parallel", "parallel"),
        name="ffn_down",
    )(g, w_out, x, mods)


def _head_rms(x, g):
    return x * lax.rsqrt(jnp.mean(x * x, axis=-1, keepdims=True) + EPS) * g


def _rope(x, cos, sin_signed):
    lane = lax.broadcasted_iota(jnp.int32, x.shape, 1)
    low = jnp.bitwise_and(lane, HEAD_DIM // 2 - 1) < (HEAD_DIM // 4)
    rot = jnp.where(low, pltpu.roll(x, HEAD_DIM - HEAD_DIM // 4, 1), pltpu.roll(x, HEAD_DIM // 4, 1))
    return x * cos + rot * sin_signed


def _epi_qk_rope(acc, cols, g_ref, cos_ref, sin_ref, *, out_scale):
    g = g_ref[...]
    cos = cos_ref[...]
    sin = sin_ref[...]
    heads = []
    for h in range(acc.shape[1] // HEAD_DIM):
        x = _head_rms(acc[:, h * HEAD_DIM:(h + 1) * HEAD_DIM], g)
        heads.append(_rope(x, cos, sin) * out_scale)
    return jnp.concatenate(heads, axis=1)


def _epi_plain(acc, cols):
    return acc


def _epi_qkv_b(acc, cols):
    return acc * jnp.where(pl.program_id(1) == 0, HEAD_DIM ** -0.5 * LOG2_E, 1.0)


def _epi_kv_a(acc, cols, g_ref, cos_ref, sin_ref):
    if cols.start < W_A_KV:
        return _epi_qk_rope(acc, cols, g_ref, cos_ref, sin_ref, out_scale=1.0)
    return acc


def _gelu_tanh(x):
    return 0.5 * x * (1.0 + jnp.tanh(0.7978845608028654 * (x + 0.044715 * (x * x * x))))


def _epi_gelu(acc, cols):
    return _gelu_tanh(acc)


def _epi_gelu_rms(acc, cols, g_ref):
    y = _gelu_tanh(acc)
    g = g_ref[:, cols]
    groups = []
    for c in range(acc.shape[1] // HEAD_DIM):
        sl = slice(c * HEAD_DIM, (c + 1) * HEAD_DIM)
        groups.append(_head_rms(y[:, sl], g[:, sl]))
    return jnp.concatenate(groups, axis=1)


def _proj_kernel(h_ref, w_ref, *rest, epilogue, chunk):
    extras, o_ref = rest[:-1], rest[-1]
    h = h_ref[...]
    for c in range(o_ref.shape[1] // chunk):
        cols = slice(c * chunk, (c + 1) * chunk)
        o_ref[:, cols] = epilogue(_dot(h, w_ref[:, cols]), cols, *extras).astype(o_ref.dtype)


def _proj(h, w_in, layer, col_off, n_cols, epilogue, extras, extra_specs, out_dtype, name, chunk=None, tn=None):
    n, d = h.shape
    tm = _tile(n, TILE_PROJ[0], BF16_SUBLANES)
    tn = _tile(n_cols, TILE_PROJ[1], LANES) if tn is None else tn
    chunk = tn if chunk is None else chunk
    assert col_off % tn == 0 and tn % chunk == 0
    off = col_off // tn
    return pl.pallas_call(
        functools.partial(_proj_kernel, epilogue=epilogue, chunk=chunk),
        grid=(n // tm, n_cols // tn),
        in_specs=[
            pl.BlockSpec((tm, d), lambda i, j: (i, 0)),
            pl.BlockSpec((None, d, tn), lambda i, j: (layer, 0, j + off)),
        ] + [spec(tm, tn) for spec in extra_specs],
        out_specs=pl.BlockSpec((tm, tn), lambda i, j: (i, j)),
        out_shape=jax.ShapeDtypeStruct((n, n_cols), out_dtype),
        compiler_params=_params("parallel", "parallel"),
        name=name,
    )(h, w_in, *extras)


def _gattn_kernel(q_ref, kt_ref, v_ref, o_ref, qs_ref, m_ref, l_ref, acc_ref, *, tq, tk, n_lat, n_ctx):
    for h in range(GQA_REP):
        qs_ref[h * tq:(h + 1) * tq, :] = q_ref[:, h * HEAD_DIM:(h + 1) * HEAD_DIM]
    m_ref[...] = jnp.full(m_ref.shape, MASK_VALUE, F32)
    l_ref[...] = jnp.zeros(l_ref.shape, F32)
    acc_ref[...] = jnp.zeros(acc_ref.shape, F32)

    def step(kt, v):
        ncol = v.shape[0] // HEAD_DIM
        s = _dot(qs_ref[...], kt)
        m_old = m_ref[...]
        m_new = jnp.maximum(m_old, jnp.max(s, axis=1, keepdims=True))
        alpha = jnp.exp2(m_old - m_new)
        p = jnp.exp2(s - jnp.concatenate([m_new] * ncol, axis=1))
        psum = p[:, :HEAD_DIM]
        for j in range(1, ncol):
            psum = psum + p[:, j * HEAD_DIM:(j + 1) * HEAD_DIM]
        l_ref[...] = alpha * l_ref[...] + psum
        acc_ref[...] = alpha * acc_ref[...] + _dot(p.astype(BF16), v)
        m_ref[...] = m_new

    def body(c, carry):
        off = pl.multiple_of(c * tk, tk)
        step(kt_ref[:, pl.ds(off, tk)], v_ref[pl.ds(off, tk), :])
        return carry

    n_chunks = (n_lat + n_ctx) // tk
    is_ctx = pl.program_id(1) * tq >= n_lat

    @pl.when(jnp.logical_not(is_ctx))
    def _():
        lax.fori_loop(0, n_chunks, body, 0, unroll=n_chunks if n_chunks <= GATTN_MAX_UNROLL else 1)

    @pl.when(is_ctx)
    def _():
        step(kt_ref[:, n_lat:n_lat + n_ctx], v_ref[n_lat:n_lat + n_ctx, :])

    o = acc_ref[...] / jnp.sum(l_ref[...], axis=1, keepdims=True)
    for h in range(GQA_REP):
        o_ref[:, h * HEAD_DIM:(h + 1) * HEAD_DIM] = o[h * tq:(h + 1) * tq].astype(o_ref.dtype)


def _global_attn(qa, ka_t, kv_a, n_lat, n_ctx):
    n = qa.shape[0]
    tq = _tile(n_ctx, GATTN_Q_ROWS, BF16_SUBLANES)
    assert n_lat % tq == 0 and n_ctx % tq == 0
    tk = _tile(n, GATTN_KV_CHUNK, V7X_MXU_COLS)
    gw = GQA_REP * HEAD_DIM
    kern = functools.partial(_gattn_kernel, tq=tq, tk=tk, n_lat=n_lat, n_ctx=n_ctx)
    return pl.pallas_call(
        kern,
        grid=(N_KV_A, n // tq),
        in_specs=[
            pl.BlockSpec((tq, gw), lambda g, i: (i, g)),
            pl.BlockSpec((HEAD_DIM, n), lambda g, i: (g, 0)),
            pl.BlockSpec((n, HEAD_DIM), lambda g, i: (0, N_KV_A + g)),
        ],
        out_specs=pl.BlockSpec((tq, gw), lambda g, i: (i, g)),
        out_shape=jax.ShapeDtypeStruct((n, W_A_Q), BF16),
        scratch_shapes=[
            pltpu.VMEM((GQA_REP * tq, HEAD_DIM), BF16),
            pltpu.VMEM((GQA_REP * tq, HEAD_DIM), F32),
            pltpu.VMEM((GQA_REP * tq, HEAD_DIM), F32),
            pltpu.VMEM((GQA_REP * tq, HEAD_DIM), F32),
        ],
        compiler_params=_params("parallel", "parallel"),
        name="global_attn",
    )(qa, ka_t, kv_a)


def _nbr_kernel(q_ref, *rest):
    k_refs = rest[:NBR_KV_BLOCKS]
    v_refs = rest[NBR_KV_BLOCKS:2 * NBR_KV_BLOCKS]
    kc_ref, vc_ref, bias_ref, o_ref = rest[2 * NBR_KV_BLOCKS:]
    nt = (((1,), (1,)), ((), ()))
    heads = [slice(h * HEAD_DIM, (h + 1) * HEAD_DIM) for h in range(N_HEADS_B)]
    scores = []
    for h, sl in enumerate(heads):
        k = jnp.concatenate([r[:, sl] for r in k_refs] + [kc_ref[:, sl]], axis=0)
        scores.append(lax.dot_general(q_ref[:, sl], k, nt, preferred_element_type=F32) + bias_ref[h])
    probs, sums = [], []
    for s in scores:
        p = jnp.exp2(s - jnp.max(s, axis=1, keepdims=True))
        sums.append(jnp.sum(p, axis=1, keepdims=True))
        probs.append(p.astype(BF16))
    for h, sl in enumerate(heads):
        v = jnp.concatenate([r[:, sl] for r in v_refs] + [vc_ref[:, sl]], axis=0)
        o_ref[:, sl] = (_dot(probs[h], v) / sums[h]).astype(o_ref.dtype)


def _nbr_bias_tables(rpb, n_lat, n_ctx):
    rows = n_lat // GRID_W
    nblk = n_lat // NBR_Q
    kh = min(WIN_H, rows)
    q_rows, kv_rows = NBR_Q // GRID_W, NBR_KEYS // GRID_W
    n_off_r, n_off_c = 2 * WIN_H - 1, 2 * WIN_W - 1
    blocks = (0, 1, 2, nblk - 2, nblk - 1)
    sel_r = np.zeros((len(blocks), n_off_r, q_rows, kv_rows), np.float32)
    for p, blk in enumerate(blocks):
        blk0 = min(max(blk - 2, 0), nblk - NBR_KV_BLOCKS)
        for qr in range(q_rows):
            rq = q_rows * blk + qr
            r0 = min(max(rq - kh // 2, 0), rows - kh)
            for kr in range(kv_rows):
                key_row = q_rows * blk0 + kr
                if r0 <= key_row < r0 + kh:
                    sel_r[p, key_row - rq + WIN_H - 1, qr, kr] = 1.0
    sel_c = np.zeros((n_off_c, GRID_W, GRID_W), np.float32)
    for qc in range(GRID_W):
        c0 = min(max(qc - WIN_W // 2, 0), GRID_W - WIN_W)
        for kc in range(c0, c0 + WIN_W):
            sel_c[kc - qc + WIN_W - 1, qc, kc] = 1.0
    inside = np.einsum("paqk,bcd->pqckd", sel_r, sel_c).reshape(len(blocks), NBR_Q, NBR_KEYS) > 0.5
    bias = jnp.einsum("lhab,paqk,bcd->lphqckd", rpb.astype(F32), sel_r, sel_c,
                      precision=lax.Precision.HIGHEST)
    bias = bias.reshape(rpb.shape[0], len(blocks), N_HEADS_B, NBR_Q, NBR_KEYS) * LOG2_E
    bias = jnp.where(inside[None, :, None], bias, MASK_VALUE)
    bias = jnp.concatenate([bias, jnp.full_like(bias[:, :1], MASK_VALUE)], axis=1)
    return jnp.pad(bias, ((0, 0),) * 4 + ((0, n_ctx),))


def _nbr_attn(qkv_b, bias, layer, n_lat, n_ctx):
    n = qkv_b.shape[0]
    nblk = n_lat // NBR_Q
    assert nblk >= NBR_KV_BLOCKS + 1 and n_ctx % NBR_Q == 0
    ctx_blk = n_lat // n_ctx
    assert n_lat % n_ctx == 0

    def first_kv(i):
        return jnp.clip(i - 2, 0, nblk - NBR_KV_BLOCKS)

    def pattern(i):
        return jnp.where(i >= nblk, NBR_PATTERNS - 1, i - first_kv(i))

    def kv_spec(col, b):
        return pl.BlockSpec((NBR_Q, W_B), lambda i: (first_kv(i) + b, col))

    in_specs = ([pl.BlockSpec((NBR_Q, W_B), lambda i: (i, 0))]
                + [kv_spec(1, b) for b in range(NBR_KV_BLOCKS)]
                + [kv_spec(2, b) for b in range(NBR_KV_BLOCKS)]
                + [pl.BlockSpec((n_ctx, W_B), lambda i: (ctx_blk, 1)),
                   pl.BlockSpec((n_ctx, W_B), lambda i: (ctx_blk, 2)),
                   pl.BlockSpec((None, None, N_HEADS_B, NBR_Q, NBR_KEYS + n_ctx),
                                lambda i: (layer, pattern(i), 0, 0, 0))])
    return pl.pallas_call(
        _nbr_kernel,
        grid=(n // NBR_Q,),
        in_specs=in_specs,
        out_specs=pl.BlockSpec((NBR_Q, W_B), lambda i: (i, 0)),
        out_shape=jax.ShapeDtypeStruct((n, W_B), BF16),
        compiler_params=_params("parallel"),
        name="nbr_attn",
    )(*([qkv_b] * (3 + 2 * NBR_KV_BLOCKS)), bias)


def _sgu_kernel(u_ref, vn_ref, w_ref, bt_ref, o_ref, *, n_chunks):
    for g in range(N_GROUPS_C):
        sl = slice(g * HEAD_DIM, (g + 1) * HEAD_DIM)
        w = w_ref[g]
        b = bt_ref[:, g:g + 1]
        for c in range(n_chunks):
            rs = slice(c * CHUNK, (c + 1) * CHUNK)
            sv = _dot(w, vn_ref[rs, sl]) + b
            o_ref[rs, sl] = (u_ref[rs, sl] * sv).astype(o_ref.dtype)


def _sgu(u, vn, w_s, b_t, layer):
    n = u.shape[0]
    tr = _tile(n, TILE_SGU_ROWS, CHUNK)
    kern = functools.partial(_sgu_kernel, n_chunks=tr // CHUNK)
    return pl.pallas_call(
        kern,
        grid=(n // tr,),
        in_specs=[
            pl.BlockSpec((tr, W_C), lambda i: (i, 0)),
            pl.BlockSpec((tr, W_C), lambda i: (i, 0)),
            pl.BlockSpec((None, N_GROUPS_C, CHUNK, CHUNK), lambda i: (layer, 0, 0, 0)),
            pl.BlockSpec((None, CHUNK, N_GROUPS_C), lambda i: (layer, 0, 0)),
        ],
        out_specs=pl.BlockSpec((tr, W_C), lambda i: (i, 0)),
        out_shape=jax.ShapeDtypeStruct((n, W_C), BF16),
        compiler_params=_params("parallel"),
        name="sgu",
    )(u, vn, w_s, b_t)


def _out_proj_kernel(oa_ref, ob_ref, oc_ref, wa_ref, wb_ref, wc_ref, x_ref, gate_ref, o_ref, *, tm, n_lat):
    y = _dot(oa_ref[...], wa_ref[...]) + _dot(ob_ref[...], wb_ref[...]) + _dot(oc_ref[...], wc_ref[...])
    o_ref[...] = x_ref[...] + _row_select(gate_ref, tm, n_lat) * y


def _out_proj(oa, ob, oc, w_out, x, mods, layer, n_lat):
    n, d = x.shape
    tm = _tile(n, TILE_OUT_PROJ[0], BF16_SUBLANES)
    tn = _tile(d, TILE_OUT_PROJ[1], LANES)
    assert W_A_Q % W_B == 0 and W_B == W_C
    kern = functools.partial(_out_proj_kernel, tm=tm, n_lat=n_lat)
    return pl.pallas_call(
        kern,
        grid=(n // tm, d // tn),
        in_specs=[
            pl.BlockSpec((tm, W_A_Q), lambda i, j: (i, 0)),
            pl.BlockSpec((tm, W_B), lambda i, j: (i, 0)),
            pl.BlockSpec((tm, W_C), lambda i, j: (i, 0)),
            pl.BlockSpec((None, W_A_Q, tn), lambda i, j: (layer, 0, j)),
            pl.BlockSpec((None, W_B, tn), lambda i, j: (layer, W_A_Q // W_B, j)),
            pl.BlockSpec((None, W_C, tn), lambda i, j: (layer, W_A_Q // W_B + 1, j)),
            pl.BlockSpec((tm, tn), lambda i, j: (i, j)),
            pl.BlockSpec((None, None, MOD_ROWS, tn), lambda i, j: (layer, 5, 0, j)),
        ],
        out_specs=pl.BlockSpec((tm, tn), lambda i, j: (i, j)),
        out_shape=jax.ShapeDtypeStruct((n, d), F32),
        compiler_params=_params("parallel", "parallel"),
        name="out_proj",
    )(oa, ob, oc, w_out, w_out, w_out, x, mods)


def _rope_tables(n_lat, n_ctx):
    t = jnp.arange(n_lat, dtype=jnp.int32)
    row = (t // GRID_W).astype(F32)
    col = (t % GRID_W).astype(F32)
    half = HEAD_DIM // 2
    inv = ROPE_THETA ** (-jnp.arange(0, half, 2, dtype=F32) / half)
    ang_r = row[:, None] * inv[None, :]
    ang_c = col[:, None] * inv[None, :]
    ang = jnp.concatenate([ang_r, ang_r, ang_c, ang_c], axis=-1)
    lane = jnp.arange(HEAD_DIM)
    sign = jnp.where((lane % half) < HEAD_DIM // 4, -1.0, 1.0).astype(F32)
    cos = jnp.concatenate([jnp.cos(ang), jnp.ones((n_ctx, HEAD_DIM), F32)], axis=0)
    sin = jnp.concatenate([jnp.sin(ang) * sign, jnp.zeros((n_ctx, HEAD_DIM), F32)], axis=0)
    return cos, sin


def _mixer_groups(h, l, mix_w_in_b, qk_g, sgu_g, cos, sin, nbr_bias, sgu_w_b, sgu_b_t, n_lat, n_ctx):
    rope_specs = [lambda tm, tn: pl.BlockSpec((tm, HEAD_DIM), lambda i, j: (i, 0))] * 2

    def g_spec(which):
        return lambda tm, tn: pl.BlockSpec((None, None, 1, HEAD_DIM), lambda i, j: (l, which, 0, 0))

    col_ka = W_A_Q
    col_b = col_ka + 2 * W_A_KV
    col_u = col_b + 3 * W_B
    col_v = col_u + W_C
    qa = _proj(h, mix_w_in_b, l, 0, W_A_Q,
               functools.partial(_epi_qk_rope, out_scale=HEAD_DIM ** -0.5 * LOG2_E),
               [qk_g, cos, sin], [g_spec(0)] + rope_specs, BF16, "proj_qa", chunk=V7X_MXU_COLS)
    kv_a = _proj(h, mix_w_in_b, l, col_ka, 2 * W_A_KV, _epi_kv_a,
                 [qk_g, cos, sin], [g_spec(1)] + rope_specs, BF16, "proj_kv_a", chunk=V7X_MXU_COLS)
    qkv_b = _proj(h, mix_w_in_b, l, col_b, 3 * W_B, _epi_qkv_b, [], [], BF16, "proj_b", tn=W_B)
    u = _proj(h, mix_w_in_b, l, col_u, W_C, _epi_gelu, [], [], F32, "proj_u", chunk=V7X_MXU_COLS)
    vn = _proj(h, mix_w_in_b, l, col_v, W_C, _epi_gelu_rms, [sgu_g],
               [lambda tm, tn: pl.BlockSpec((None, 1, tn), lambda i, j: (l, 0, j))], BF16, "proj_v",
               chunk=V7X_MXU_COLS)
    oa = _global_attn(qa, kv_a[:, :W_A_KV].T, kv_a, n_lat, n_ctx)
    ob = _nbr_attn(qkv_b, nbr_bias, l, n_lat, n_ctx)
    oc = _sgu(u, vn, sgu_w_b, sgu_b_t, l)
    return oa, ob, oc


def kernel(x, c, ctx, c_ctx, norm_g, mod_w_down, mod_w_up, mod_b, ffn_w_in, ffn_w_out, mix_w_in, mix_w_out,
           qk_norm_g, rpb, sgu_norm_g, sgu_w, sgu_b, final_norm_g):
    batch, n_lat, d = x.shape
    n_ctx = ctx.shape[1]
    depth = norm_g.shape[0]
    assert batch == 1 and c.shape[0] == 1

    ffn_w_out_b = ffn_w_out.astype(BF16)
    mix_w_in_b = mix_w_in.astype(BF16)
    mix_w_out_b = mix_w_out.astype(BF16)
    sgu_w_b = sgu_w.astype(BF16)
    sgu_b_t = jnp.swapaxes(sgu_b, 1, 2)
    norm_g4 = norm_g.reshape(depth, 3, 1, d)

    cond = jnp.zeros((MOD_ROWS, d), F32).at[0].set(c[0]).at[1].set(c_ctx)
    mods = _adaln(cond, mod_w_down, mod_w_up, mod_b)
    mods = mods.reshape(depth, MOD_ROWS, N_MOD, d).swapaxes(1, 2)

    cos, sin = _rope_tables(n_lat, n_ctx)
    nbr_bias = _nbr_bias_tables(rpb, n_lat, n_ctx)
    qk_g = qk_norm_g.reshape(depth, 2, 1, HEAD_DIM)
    sgu_g = sgu_norm_g.reshape(depth, 1, W_C)

    xs = jnp.concatenate([x[0], ctx[0]], axis=0)

    for l in range(depth):
        h = _norm_mod(xs, norm_g4, mods, l, 0, n_lat)
        xs = _ffn_down(_ffn_up(h, ffn_w_in, l, 0), ffn_w_out_b, xs, mods, l, 0, 2, n_lat)

        h = _norm_mod(xs, norm_g4, mods, l, 1, n_lat)
        oa, ob, oc = _mixer_groups(h, l, mix_w_in_b, qk_g, sgu_g, cos, sin, nbr_bias, sgu_w_b, sgu_b_t,
                                   n_lat, n_ctx)
        xs = _out_proj(oa, ob, oc, mix_w_out_b, xs, mods, l, n_lat)

        h = _norm_mod(xs, norm_g4, mods, l, 2, n_lat)
        xs = _ffn_down(_ffn_up(h, ffn_w_in, l, 1), ffn_w_out_b, xs, mods, l, 1, 8, n_lat)

    return _final_norm(xs, final_norm_g, n_lat).reshape(1, n_lat, d)
```
